```python
import math
import jax, jax.numpy as jnp
from jax import lax
import numpy as np

D_MODEL = 1024
BATCH = 4
SEQ = 8192
DEPTH = 2
DEC_BATCH = 16
DEC_SEQ = 4096
PAST_LEN = 128

GRID_W = 64
HEAD_DIM = 64
NA_HEADS = 4
NA_ROWS = 8
NA_COLS = 16
NA_D = NA_HEADS * HEAD_DIM
MLA_HEADS = 6
MLA_Q_RANK = 384
MLA_KV_RANK = 256
MLA_NOPE = 64
MLA_ROPE = 32
MLA_V = 64
MLA_QK = MLA_NOPE + MLA_ROPE
GQA_HEADS = 6
GQA_KV_HEADS = 2
GQA_GROUP = GQA_HEADS // GQA_KV_HEADS
MIX_WIDTH = NA_D + MLA_HEADS * MLA_V + GQA_HEADS * HEAD_DIM
IN_SIZES = (NA_D, NA_D, NA_D,
            MLA_Q_RANK, MLA_KV_RANK, MLA_ROPE,
            GQA_HEADS * HEAD_DIM, GQA_KV_HEADS * HEAD_DIM, GQA_KV_HEADS * HEAD_DIM)
IN_WIDTH = sum(IN_SIZES)
IN_SPLITS = [int(v) for v in np.cumsum(IN_SIZES)[:-1]]
D_FF = ((-(-8 * D_MODEL // 3)) + 255) // 256 * 256
Q_BLOCK = 128
ROPE_BASE = 10000.0
EPS = 1e-6

kernel_name = "hybrid_natten_mla_axialgqa_encoder"


def rms_norm(x, g):
    xf = x.astype(jnp.float32)
    y = xf * lax.rsqrt(jnp.mean(xf * xf, axis=-1, keepdims=True) + EPS)
    return (y * g.astype(jnp.float32)).astype(x.dtype)


def rope(x, pos):
    d = x.shape[-1]
    half = d // 2
    inv = ROPE_BASE ** (-(jnp.arange(half, dtype=jnp.float32) * 2.0) / d)
    ang = pos[:, None] * inv[None, :]
    cos = jnp.cos(ang)[None, :, None, :]
    sin = jnp.sin(ang)[None, :, None, :]
    xf = x.astype(jnp.float32)
    x1, x2 = xf[..., :half], xf[..., half:]
    return jnp.concatenate([x1 * cos - x2 * sin, x2 * cos + x1 * sin], axis=-1).astype(x.dtype)


def axial_rope(x, row, col):
    a = x.shape[-1] // 2
    return jnp.concatenate([rope(x[..., :a], row), rope(x[..., a:], col)], axis=-1)


def block_attention(q, k, v, scale):
    B, S, Hk, G, Dq = q.shape
    Dv = v.shape[-1]
    nb = S // Q_BLOCK
    qb = q.reshape(B, nb, Q_BLOCK, Hk, G, Dq).transpose(1, 0, 2, 3, 4, 5)

    def one_block(qi):
        s = jnp.einsum('bqhgd,bkhd->bhgqk', qi, k).astype(jnp.float32) * scale
        p = jax.nn.softmax(s, axis=-1)
        return jnp.einsum('bhgqk,bkhd->bqhgd', p.astype(v.dtype), v)

    o = lax.map(one_block, qb)
    return o.transpose(1, 0, 2, 3, 4, 5).reshape(B, S, Hk * G, Dv)


def neighbourhood_attention(q, k, v, rpb):
    B, S, H, D = q.shape
    rows = S // GRID_W
    kr = min(NA_ROWS, rows)
    scale = 1.0 / math.sqrt(D)
    qg = q.reshape(B, rows, GRID_W, H, D)
    kg = k.reshape(B, rows, GRID_W, H, D)
    vg = v.reshape(B, rows, GRID_W, H, D)
    cols = jnp.arange(GRID_W)
    col_start = jnp.clip(cols - NA_COLS // 2, 0, GRID_W - NA_COLS)
    col_idx = col_start[:, None] + jnp.arange(NA_COLS)[None, :]
    dc = col_idx - cols[:, None] + (NA_COLS - 1)
    bias_c = rpb.astype(jnp.float32)[:, :, dc]

    def one_row(r):
        rs = jnp.clip(r - kr // 2, 0, rows - kr)
        qr = lax.dynamic_index_in_dim(qg, r, axis=1, keepdims=False)
        kb = lax.dynamic_slice_in_dim(kg, rs, kr, axis=1)
        vb = lax.dynamic_slice_in_dim(vg, rs, kr, axis=1)
        kw = kb[:, :, col_idx]
        vw = vb[:, :, col_idx]
        s = jnp.einsum('bchd,bjckhd->bhcjk', qr, kw).astype(jnp.float32) * scale
        dr = rs + jnp.arange(kr) - r + (NA_ROWS - 1)
        bias = jnp.take(bias_c, dr, axis=1).transpose(0, 2, 1, 3)
        s = (s + bias[None]).reshape(B, H, GRID_W, kr * NA_COLS)
        p = jax.nn.softmax(s, axis=-1).reshape(B, H, GRID_W, kr, NA_COLS)
        return jnp.einsum('bhcjk,bjckhd->bchd', p.astype(v.dtype), vw)

    o = lax.map(one_row, jnp.arange(rows))
    return o.transpose(1, 0, 2, 3, 4).reshape(B, S, H, D)


def hybrid_mixer(h, w_in, q_a_norm, kv_a_norm, w_uq, w_ukv, q_norm_c, k_norm_c, rpb,
                 g_out_a, g_out_b, g_out_c, w_out):
    B, S, _ = h.shape
    tok = jnp.arange(S)
    t = tok.astype(jnp.float32)
    row = (tok // GRID_W).astype(jnp.float32)
    col = (tok % GRID_W).astype(jnp.float32)

    proj = h @ w_in
    (qa, ka, va, cq, ckv, kpe, qc, kc, vc) = jnp.split(proj, IN_SPLITS, axis=-1)

    oa = neighbourhood_attention(qa.reshape(B, S, NA_HEADS, HEAD_DIM),
                                 ka.reshape(B, S, NA_HEADS, HEAD_DIM),
                                 va.reshape(B, S, NA_HEADS, HEAD_DIM), rpb)
    oa = oa.reshape(B, S, NA_D)

    qb = (rms_norm(cq, q_a_norm) @ w_uq).reshape(B, S, MLA_HEADS, MLA_QK)
    qb = jnp.concatenate([qb[..., :MLA_NOPE], rope(qb[..., MLA_NOPE:], t)], axis=-1)
    kv = (rms_norm(ckv, kv_a_norm) @ w_ukv).reshape(B, S, MLA_HEADS, MLA_NOPE + MLA_V)
    k_nope, vb = kv[..., :MLA_NOPE], kv[..., MLA_NOPE:]
    k_pe = rope(kpe.reshape(B, S, 1, MLA_ROPE), t)
    kb = jnp.concatenate([k_nope, jnp.broadcast_to(k_pe, (B, S, MLA_HEADS, MLA_ROPE))], axis=-1)
    ob = block_attention(qb[:, :, :, None, :], kb, vb, 1.0 / math.sqrt(MLA_QK))
    ob = ob.reshape(B, S, MLA_HEADS * MLA_V)

    qc = axial_rope(rms_norm(qc.reshape(B, S, GQA_HEADS, HEAD_DIM), q_norm_c), row, col)
    kc = axial_rope(rms_norm(kc.reshape(B, S, GQA_KV_HEADS, HEAD_DIM), k_norm_c), row, col)
    vc = vc.reshape(B, S, GQA_KV_HEADS, HEAD_DIM)
    oc = block_attention(qc.reshape(B, S, GQA_KV_HEADS, GQA_GROUP, HEAD_DIM), kc, vc,
                         1.0 / math.sqrt(HEAD_DIM))
    oc = oc.reshape(B, S, GQA_HEADS * HEAD_DIM)

    merged = jnp.concatenate([rms_norm(oa, g_out_a), rms_norm(ob, g_out_b),
                              rms_norm(oc, g_out_c)], axis=-1)
    return merged @ w_out


def swiglu(h, w_gate, w_up, w_down):
    return (jax.nn.silu(h @ w_gate) * (h @ w_up)) @ w_down


def run_trunk(x, attn_norm, w_in, q_a_norm, kv_a_norm, w_uq, w_ukv, q_norm_c, k_norm_c, rpb,
              g_out_a, g_out_b, g_out_c, w_out, ffn_norm, w_gate, w_up, w_down, final_norm):
    for l in range(DEPTH):
        h = rms_norm(x, attn_norm[l])
        x = x + hybrid_mixer(h, w_in[l], q_a_norm[l], kv_a_norm[l], w_uq[l], w_ukv[l],
                             q_norm_c[l], k_norm_c[l], rpb[l],
                             g_out_a[l], g_out_b[l], g_out_c[l], w_out[l])
        h = rms_norm(x, ffn_norm[l])
        x = x + swiglu(h, w_gate[l], w_up[l], w_down[l])
    return rms_norm(x, final_norm)


def setup_inputs(seed: int = 0) -> dict:
    key = jax.random.key(seed)
    ks = jax.random.split(key, 24)

    def w(k, shape, fan_in):
        return jax.random.normal(k, shape, jnp.float32) * (fan_in ** -0.5)

    def gain(k, shape):
        return 1.0 + 0.02 * jax.random.normal(k, shape, jnp.float32)

    L = DEPTH
    return {
        "x_prompt": jax.random.normal(ks[0], (BATCH, SEQ, D_MODEL), jnp.float32),
        "x_sample": jax.random.normal(ks[1], (DEC_BATCH, DEC_SEQ, D_MODEL), jnp.float32),
        "attn_norm": gain(ks[2], (L, D_MODEL)),
        "w_in": w(ks[3], (L, D_MODEL, IN_WIDTH), D_MODEL),
        "q_a_norm": gain(ks[4], (L, MLA_Q_RANK)),
        "kv_a_norm": gain(ks[5], (L, MLA_KV_RANK)),
        "w_uq": w(ks[6], (L, MLA_Q_RANK, MLA_HEADS * MLA_QK), MLA_Q_RANK),
        "w_ukv": w(ks[7], (L, MLA_KV_RANK, MLA_HEADS * (MLA_NOPE + MLA_V)), MLA_KV_RANK),
        "q_norm_c": gain(ks[8], (L, HEAD_DIM)),
        "k_norm_c": gain(ks[9], (L, HEAD_DIM)),
        "rpb": 0.1 * jax.random.normal(ks[10], (L, NA_HEADS, 2 * NA_ROWS - 1, 2 * NA_COLS - 1), jnp.float32),
        "g_out_a": gain(ks[11], (L, NA_D)),
        "g_out_b": gain(ks[12], (L, MLA_HEADS * MLA_V)),
        "g_out_c": gain(ks[13], (L, GQA_HEADS * HEAD_DIM)),
        "w_out": w(ks[14], (L, MIX_WIDTH, D_MODEL), MIX_WIDTH),
        "ffn_norm": gain(ks[15], (L, D_MODEL)),
        "w_gate": w(ks[16], (L, D_MODEL, D_FF), D_MODEL),
        "w_up": w(ks[17], (L, D_MODEL, D_FF), D_MODEL),
        "w_down": w(ks[18], (L, D_FF, D_MODEL), D_FF),
        "final_norm": gain(ks[19], (D_MODEL,)),
    }


def reference(x_prompt, x_sample, attn_norm, w_in, q_a_norm, kv_a_norm, w_uq, w_ukv,
              q_norm_c, k_norm_c, rpb, g_out_a, g_out_b, g_out_c, w_out, ffn_norm,
              w_gate, w_up, w_down, final_norm):
    y_prompt = run_trunk(x_prompt, attn_norm, w_in, q_a_norm, kv_a_norm, w_uq, w_ukv,
                         q_norm_c, k_norm_c, rpb, g_out_a, g_out_b, g_out_c, w_out,
                         ffn_norm, w_gate, w_up, w_down, final_norm)
    y_sample = run_trunk(x_sample, attn_norm, w_in, q_a_norm, kv_a_norm, w_uq, w_ukv,
                         q_norm_c, k_norm_c, rpb, g_out_a, g_out_b, g_out_c, w_out,
                         ffn_norm, w_gate, w_up, w_down, final_norm)
    return (y_prompt, y_sample)
```

```python
import functools
import math

import jax
import jax.numpy as jnp
import numpy as np
from jax import lax
from jax.experimental import pallas as pl
from jax.experimental.pallas import tpu as pltpu

D_MODEL = 1024
GRID_W = 64
HEAD_DIM = 64
NA_HEADS = 4
NA_ROWS = 8
NA_COLS = 16
NA_D = NA_HEADS * HEAD_DIM
MLA_HEADS = 6
MLA_Q_RANK = 384
MLA_KV_RANK = 256
MLA_NOPE = 64
MLA_ROPE = 32
MLA_V = 64
MLA_QK = MLA_NOPE + MLA_ROPE
GQA_HEADS = 6
GQA_KV_HEADS = 2
GQA_GROUP = GQA_HEADS // GQA_KV_HEADS
D_FF = 2816
ROPE_BASE = 10000.0
EPS = 1e-6

LANES = 128
VMEM_LIMIT_BYTES = 56 * 1024 * 1024

PRE_TM = 512
ATTN_TQ = 256
ATTN_TK = PRE_TM
NA_RB = 8
NA_KEYS = NA_ROWS * GRID_W
POST_TM = 256

LOG2E = 1.4426950408889634
MLA_QSCALE = LOG2E / math.sqrt(MLA_QK)
GQA_QSCALE = LOG2E / math.sqrt(HEAD_DIM)
NA_QSCALE = 1.0 / math.sqrt(HEAD_DIM)
MASK_VALUE = -1e30

BF16 = jnp.bfloat16
F32 = jnp.float32
NT_DIMS = (((1,), (1,)), ((), ()))


def _dot(a, b):
    return jnp.dot(a, b, preferred_element_type=F32)


def _dot_nt(a, b):
    return lax.dot_general(a, b, NT_DIMS, preferred_element_type=F32)


def _rms(x, g):
    return x * lax.rsqrt(jnp.mean(x * x, axis=-1, keepdims=True) + EPS) * g


def _compiler_params(n_axes):
    return pltpu.CompilerParams(
        dimension_semantics=("arbitrary",) * n_axes,
        vmem_limit_bytes=VMEM_LIMIT_BYTES,
    )


def _const_spec(shape):
    nd = len(shape)
    return pl.BlockSpec(shape, lambda *_: (0,) * nd, pipeline_mode=pl.Buffered(1))


def _pre_kernel(x_ref, gattn_ref, wna_ref, wmla_ref, wqc_ref, wkc_ref, wvct_ref,
                qan_ref, kvan_ref, wuq_ref, wuk_ref, wuvt_ref, gq_ref, gk_ref,
                cosm_ref, sinm_ref, cosa_ref, sina_ref,
                na_ref, mq_ref, mk_ref, mvt_ref, gqo_ref, gko_ref, gvt_ref):
    x = x_ref[0]
    xb = _rms(x, gattn_ref[...]).astype(BF16)

    na_ref[0] = _dot(xb, wna_ref[...]).astype(BF16)

    cosm = cosm_ref[...]
    sinm = sinm_ref[...]
    cosa = cosa_ref[...]
    sina = sina_ref[...]

    lat = _dot(xb, wmla_ref[...])
    cq = lat[:, :MLA_Q_RANK]
    ckv = lat[:, MLA_Q_RANK:MLA_Q_RANK + MLA_KV_RANK]
    o = MLA_Q_RANK + MLA_KV_RANK
    kpe = lat[:, o:o + LANES] * cosm + lat[:, o + LANES:o + 2 * LANES] * sinm

    cqn = _rms(cq, qan_ref[...]).astype(BF16)
    q2 = _dot(cqn, wuq_ref[...])
    hw = MLA_HEADS * LANES
    for h in range(MLA_HEADS):
        sl = slice(h * LANES, (h + 1) * LANES)
        qh = q2[:, sl] * cosm + q2[:, hw + h * LANES:hw + (h + 1) * LANES] * sinm
        mq_ref[0, :, sl] = (qh * MLA_QSCALE).astype(BF16)

    ckvn = _rms(ckv, kvan_ref[...]).astype(BF16)
    kn = _dot(ckvn, wuk_ref[...])
    for h in range(MLA_HEADS):
        sl = slice(h * LANES, (h + 1) * LANES)
        mk_ref[0, :, sl] = (kn[:, sl] + kpe).astype(BF16)
    mvt_ref[0, 0] = _dot_nt(wuvt_ref[...], ckvn).astype(BF16)

    gq = gq_ref[...]
    gqc = cosa * gq[0:1]
    gqs = sina * gq[1:2]
    qc2 = _dot(xb, wqc_ref[...])
    hw = GQA_HEADS * LANES
    for h in range(GQA_HEADS):
        sl = slice(h * LANES, (h + 1) * LANES)
        xh = qc2[:, sl]
        xr = qc2[:, hw + h * LANES:hw + (h + 1) * LANES]
        ms = jnp.sum(xh * xh, axis=-1, keepdims=True) * (1.0 / HEAD_DIM)
        sc = lax.rsqrt(ms + EPS) * GQA_QSCALE
        gqo_ref[0, :, sl] = ((xh * gqc + xr * gqs) * sc).astype(BF16)

    gk = gk_ref[...]
    kc2 = _dot(xb, wkc_ref[...])
    xk = kc2[:, :LANES]
    xkr = kc2[:, LANES:]
    lane = lax.broadcasted_iota(jnp.int32, xk.shape, 1)
    first = lane < HEAD_DIM
    sq = xk * xk
    ms0 = jnp.sum(jnp.where(first, sq, 0.0), axis=-1, keepdims=True) * (1.0 / HEAD_DIM)
    ms1 = jnp.sum(jnp.where(first, 0.0, sq), axis=-1, keepdims=True) * (1.0 / HEAD_DIM)
    sc = jnp.where(first, lax.rsqrt(ms0 + EPS), lax.rsqrt(ms1 + EPS))
    gko_ref[0] = ((xk * (cosa * gk[0:1]) + xkr * (sina * gk[1:2])) * sc).astype(BF16)
    gvt_ref[0, 0] = _dot_nt(wvct_ref[...], xb).astype(BF16)


def _pre_call(x, lw, tabs):
    B, S, _ = x.shape
    tm = PRE_TM
    nt = S // tm
    tile = lambda w: pl.BlockSpec((1, tm, w), lambda b, i: (b, i, 0))
    tab = pl.BlockSpec((tm, LANES), lambda b, i: (i, 0))
    weights = [lw["g_attn"], lw["w_na"], lw["w_mla"], lw["w_qc"], lw["w_kc"], lw["w_vct"],
               lw["g_qa"], lw["g_kva"], lw["w_uq"], lw["w_uk"], lw["w_uvt"], lw["g_q"], lw["g_k"]]
    out_shape = (
        jax.ShapeDtypeStruct((B, S, 3 * NA_D), BF16),
        jax.ShapeDtypeStruct((B, S, MLA_HEADS * LANES), BF16),
        jax.ShapeDtypeStruct((B, S, MLA_HEADS * LANES), BF16),
        jax.ShapeDtypeStruct((B, nt, MLA_HEADS * MLA_V, tm), BF16),
        jax.ShapeDtypeStruct((B, S, GQA_HEADS * LANES), BF16),
        jax.ShapeDtypeStruct((B, S, GQA_KV_HEADS * HEAD_DIM), BF16),
        jax.ShapeDtypeStruct((B, nt, GQA_KV_HEADS * HEAD_DIM, tm), BF16),
    )
    out_specs = (
        tile(3 * NA_D), tile(MLA_HEADS * LANES), tile(MLA_HEADS * LANES),
        pl.BlockSpec((1, 1, MLA_HEADS * MLA_V, tm), lambda b, i: (b, i, 0, 0)),
        tile(GQA_HEADS * LANES), tile(GQA_KV_HEADS * HEAD_DIM),
        pl.BlockSpec((1, 1, GQA_KV_HEADS * HEAD_DIM, tm), lambda b, i: (b, i, 0, 0)),
    )
    return pl.pallas_call(
        _pre_kernel,
        grid=(B, nt),
        in_specs=[tile(D_MODEL)] + [_const_spec(w.shape) for w in weights] + [tab] * 4,
        out_specs=out_specs,
        out_shape=out_shape,
        compiler_params=_compiler_params(2),
        name="pre",
    )(x, *weights, tabs["cos_m"], tabs["sin_m"], tabs["cos_a"], tabs["sin_a"])


def _attn_kernel(q_ref, k_ref, vt_ref, o_ref, *, n_heads, k_tiles, v_rows, n_chunks, tk):
    tq = q_ref.shape[1]
    outs = []
    for h in range(n_heads):
        qh = q_ref[0, :, h * LANES:(h + 1) * LANES]
        kt = k_tiles[h]
        vr = v_rows[h]

        def body(j, carry, qh=qh, kt=kt, vr=vr):
            m, l, acc = carry
            start = pl.multiple_of(j * tk, tk)
            kc = k_ref[0, pl.ds(start, tk), kt * LANES:(kt + 1) * LANES]
            st = _dot_nt(kc, qh)
            m_new = jnp.maximum(m, jnp.max(st, axis=0, keepdims=True))
            alpha = jnp.exp2(m - m_new)
            p = jnp.exp2(st - m_new)
            l_new = alpha * l + jnp.sum(p, axis=0, keepdims=True)
            vc = vt_ref[0, j, vr * HEAD_DIM:(vr + 1) * HEAD_DIM, :]
            acc_new = alpha * acc + _dot(vc, p.astype(BF16))
            return m_new, l_new, acc_new

        m0 = jnp.full((1, tq), -jnp.inf, F32)
        l0 = jnp.zeros((1, tq), F32)
        a0 = jnp.zeros((HEAD_DIM, tq), F32)
        _, l, acc = lax.fori_loop(0, n_chunks, body, (m0, l0, a0))
        outs.append(acc / l)
    o_ref[0] = jnp.concatenate(outs, axis=0).T


def _attn_call(q, k, vt, *, k_tiles, v_rows, name):
    B, S, qw = q.shape
    n_heads = qw // LANES
    tq, tk = ATTN_TQ, ATTN_TK
    n_chunks = S // tk
    kw = k.shape[2]
    vw = vt.shape[2]
    kern = functools.partial(_attn_kernel, n_heads=n_heads, k_tiles=k_tiles, v_rows=v_rows,
                             n_chunks=n_chunks, tk=tk)
    return pl.pallas_call(
        kern,
        grid=(B, S // tq),
        in_specs=[
            pl.BlockSpec((1, tq, qw), lambda b, i: (b, i, 0)),
            pl.BlockSpec((1, S, kw), lambda b, i: (b, 0, 0)),
            pl.BlockSpec((1, n_chunks, vw, tk), lambda b, i: (b, 0, 0, 0)),
        ],
        out_specs=pl.BlockSpec((1, tq, n_heads * HEAD_DIM), lambda b, i: (b, i, 0)),
        out_shape=jax.ShapeDtypeStruct((B, S, n_heads * HEAD_DIM), F32),
        compiler_params=_compiler_params(2),
        name=name,
    )(q, k, vt)


def _na_kernel(na_ref, bias_ref, o_ref, *, rows, rb):
    i = pl.program_id(1)
    lane = lax.broadcasted_iota(jnp.int32, (GRID_W, LANES), 1)
    first = lane < HEAD_DIM

    def row_body(rr, carry):
        r = i * rb + rr
        rs = jnp.clip(r - NA_ROWS // 2, 0, rows - NA_ROWS)
        delta = r - rs
        q = na_ref[0, pl.ds(pl.multiple_of(r * GRID_W, GRID_W), GRID_W), 0:NA_D]
        kv0 = pl.multiple_of(rs * GRID_W, GRID_W)
        for t in range(NA_HEADS // 2):
            qt = q[:, t * LANES:(t + 1) * LANES]
            kt = na_ref[0, pl.ds(kv0, NA_KEYS), NA_D + t * LANES:NA_D + (t + 1) * LANES]
            vt = na_ref[0, pl.ds(kv0, NA_KEYS), 2 * NA_D + t * LANES:2 * NA_D + (t + 1) * LANES]
            pair = []
            for e in range(2):
                keep = first if e == 0 else jnp.logical_not(first)
                qm = jnp.where(keep, qt, jnp.zeros_like(qt))
                s = _dot_nt(qm, kt) + bias_ref[2 * t + e, delta]
                m = jnp.max(s, axis=-1, keepdims=True)
                p = jnp.exp(s - m)
                l = jnp.sum(p, axis=-1, keepdims=True)
                pair.append(_dot(p.astype(BF16), vt) / l)
            o_ref[0, pl.ds(pl.multiple_of(rr * GRID_W, GRID_W), GRID_W),
                  t * LANES:(t + 1) * LANES] = jnp.where(first, pair[0], pair[1])
        return carry

    lax.fori_loop(0, rb, row_body, 0)


def _na_call(na, bias):
    B, S, w = na.shape
    rows = S // GRID_W
    rb = NA_RB
    kern = functools.partial(_na_kernel, rows=rows, rb=rb)
    return pl.pallas_call(
        kern,
        grid=(B, rows // rb),
        in_specs=[
            pl.BlockSpec((1, S, w), lambda b, i: (b, 0, 0)),
            _const_spec(bias.shape),
        ],
        out_specs=pl.BlockSpec((1, rb * GRID_W, NA_D), lambda b, i: (b, i, 0)),
        out_shape=jax.ShapeDtypeStruct((B, S, NA_D), F32),
        compiler_params=_compiler_params(2),
        name="na",
    )(na, bias)


def _post_kernel(x_ref, oa_ref, ob_ref, oc_ref, ga_ref, gb_ref, gc_ref, wout_ref, gffn_ref,
                 wg_ref, wu_ref, wd_ref, gfin_ref, o_ref, *, final):
    merged = jnp.concatenate(
        [_rms(oa_ref[...], ga_ref[...]), _rms(ob_ref[...], gb_ref[...]),
         _rms(oc_ref[...], gc_ref[...])], axis=-1).astype(BF16)
    x1 = x_ref[...] + _dot(merged, wout_ref[...])
    hb = _rms(x1, gffn_ref[...]).astype(BF16)
    g = _dot(hb, wg_ref[...])
    u = _dot(hb, wu_ref[...])
    act = (g * jax.nn.sigmoid(g) * u).astype(BF16)
    x2 = x1 + _dot(act, wd_ref[...])
    if final:
        x2 = _rms(x2, gfin_ref[...])
    o_ref[...] = x2


def _post_call(x, oa, ob, oc, lw, g_final, final):
    B, S, _ = x.shape
    T = B * S
    tm = POST_TM
    flat = lambda a: a.reshape(T, a.shape[-1])
    tile = lambda w: pl.BlockSpec((tm, w), lambda i: (i, 0))
    weights_a = [lw["g_a"], lw["g_b"], lw["g_c"], lw["w_out"], lw["g_ffn"],
                 lw["w_gate"], lw["w_up"], lw["w_down"], g_final]
    out = pl.pallas_call(
        functools.partial(_post_kernel, final=final),
        grid=(T // tm,),
        in_specs=[tile(D_MODEL), tile(NA_D), tile(MLA_HEADS * MLA_V), tile(GQA_HEADS * HEAD_DIM)]
        + [_const_spec(w.shape) for w in weights_a],
        out_specs=tile(D_MODEL),
        out_shape=jax.ShapeDtypeStruct((T, D_MODEL), F32),
        compiler_params=_compiler_params(1),
        name="post",
    )(flat(x), flat(oa), flat(ob), flat(oc), *weights_a)
    return out.reshape(B, S, D_MODEL)


def _rot32(w):
    return jnp.concatenate([-w[:, 16:32], w[:, 0:16]], axis=1)


def _rot64(w):
    return jnp.concatenate([_rot32(w[:, :32]), _rot32(w[:, 32:])], axis=1)


def _perm64(g):
    p = lambda v: jnp.concatenate([v[16:32], v[0:16]])
    return jnp.concatenate([p(g[:32]), p(g[32:])])


def _layer_weights(l, attn_norm, w_in, q_a_norm, kv_a_norm, w_uq, w_ukv, q_norm_c, k_norm_c,
                   g_out_a, g_out_b, g_out_c, w_out, ffn_norm, w_gate, w_up, w_down):
    wi = w_in[l]
    K = wi.shape[0]
    sizes = (NA_D, NA_D, NA_D, MLA_Q_RANK, MLA_KV_RANK, MLA_ROPE,
             GQA_HEADS * HEAD_DIM, GQA_KV_HEADS * HEAD_DIM, GQA_KV_HEADS * HEAD_DIM)
    offs = np.concatenate([[0], np.cumsum(sizes)])
    (wqa, wka, wva, wcq, wckv, wkpe, wqc, wkc, wvc) = [wi[:, offs[i]:offs[i + 1]] for i in range(9)]
    z = lambda n: jnp.zeros((K, n), F32)

    w_na = jnp.concatenate([wqa * NA_QSCALE, wka, wva], axis=1)
    kpe_tile = jnp.concatenate([z(MLA_NOPE), wkpe, z(LANES - MLA_QK)], axis=1)
    kpe_rot = jnp.concatenate([z(MLA_NOPE), _rot32(wkpe), z(LANES - MLA_QK)], axis=1)
    w_mla = jnp.concatenate([wcq, wckv, kpe_tile, kpe_rot], axis=1)

    qc_tiles, qc_rots = [], []
    for h in range(GQA_HEADS):
        wh = wqc[:, h * HEAD_DIM:(h + 1) * HEAD_DIM]
        pad = (lambda a: jnp.concatenate([a, z(HEAD_DIM)], axis=1)) if h // GQA_GROUP == 0 else \
              (lambda a: jnp.concatenate([z(HEAD_DIM), a], axis=1))
        qc_tiles.append(pad(wh))
        qc_rots.append(pad(_rot64(wh)))
    w_qc = jnp.concatenate(qc_tiles + qc_rots, axis=1)
    w_kc = jnp.concatenate(
        [wkc] + [_rot64(wkc[:, h * HEAD_DIM:(h + 1) * HEAD_DIM]) for h in range(GQA_KV_HEADS)], axis=1)

    uq = w_uq[l]
    zq = lambda n: jnp.zeros((MLA_Q_RANK, n), F32)
    uq_tiles, uq_rots = [], []
    for h in range(MLA_HEADS):
        wh = uq[:, h * MLA_QK:(h + 1) * MLA_QK]
        uq_tiles.append(jnp.concatenate([wh, zq(LANES - MLA_QK)], axis=1))
        uq_rots.append(jnp.concatenate([zq(MLA_NOPE), _rot32(wh[:, MLA_NOPE:]), zq(LANES - MLA_QK)], axis=1))
    w_uq_ext = jnp.concatenate(uq_tiles + uq_rots, axis=1)

    ukv = w_ukv[l]
    zk = jnp.zeros((MLA_KV_RANK, LANES - MLA_NOPE), F32)
    per = MLA_NOPE + MLA_V
    w_uk = jnp.concatenate(
        [jnp.concatenate([ukv[:, h * per:h * per + MLA_NOPE], zk], axis=1) for h in range(MLA_HEADS)], axis=1)
    w_uvt = jnp.concatenate([ukv[:, h * per + MLA_NOPE:(h + 1) * per] for h in range(MLA_HEADS)], axis=1).T

    two = lambda g: jnp.concatenate([g, g])
    row = lambda g: g.reshape(1, -1).astype(F32)
    return {
        "g_attn": row(attn_norm[l]),
        "w_na": w_na.astype(BF16), "w_mla": w_mla.astype(BF16), "w_qc": w_qc.astype(BF16),
        "w_kc": w_kc.astype(BF16), "w_vct": wvc.T.astype(BF16),
        "g_qa": row(q_a_norm[l]), "g_kva": row(kv_a_norm[l]),
        "w_uq": w_uq_ext.astype(BF16), "w_uk": w_uk.astype(BF16), "w_uvt": w_uvt.astype(BF16),
        "g_q": jnp.stack([two(q_norm_c[l]), two(_perm64(q_norm_c[l]))]).astype(F32),
        "g_k": jnp.stack([two(k_norm_c[l]), two(_perm64(k_norm_c[l]))]).astype(F32),
        "g_a": row(g_out_a[l]), "g_b": row(g_out_b[l]), "g_c": row(g_out_c[l]),
        "w_out": w_out[l].astype(BF16), "g_ffn": row(ffn_norm[l]),
        "w_gate": w_gate[l].astype(BF16), "w_up": w_up[l].astype(BF16), "w_down": w_down[l].astype(BF16),
    }


def _rope_tables(S):
    tok = jnp.arange(S)
    t = tok.astype(F32)
    row = (tok // GRID_W).astype(F32)
    col = (tok % GRID_W).astype(F32)
    half = MLA_ROPE // 2
    inv = ROPE_BASE ** (-(jnp.arange(half, dtype=F32) * 2.0) / MLA_ROPE)

    def cs(pos):
        ang = pos[:, None] * inv[None, :]
        c, s = jnp.cos(ang), jnp.sin(ang)
        return jnp.concatenate([c, c], axis=1), jnp.concatenate([s, s], axis=1)

    ct, st = cs(t)
    ones = jnp.ones((S, MLA_NOPE), F32)
    zeros = jnp.zeros((S, MLA_NOPE), F32)
    pad1 = jnp.ones((S, LANES - MLA_QK), F32)
    pad0 = jnp.zeros((S, LANES - MLA_QK), F32)
    cr, sr = cs(row)
    cc, sc = cs(col)
    ca = jnp.concatenate([cr, cc], axis=1)
    sa = jnp.concatenate([sr, sc], axis=1)
    return {
        "cos_m": jnp.concatenate([ones, ct, pad1], axis=1),
        "sin_m": jnp.concatenate([zeros, st, pad0], axis=1),
        "cos_a": jnp.concatenate([ca, ca], axis=1),
        "sin_a": jnp.concatenate([sa, sa], axis=1),
    }


def _na_bias_table(rpb_l):
    j = np.arange(NA_ROWS)
    delta = np.arange(NA_ROWS)
    dr = j[None, :] - delta[:, None] + (NA_ROWS - 1)
    c = np.arange(GRID_W)
    cstart = np.clip(c - NA_COLS // 2, 0, GRID_W - NA_COLS)
    cp = np.arange(GRID_W)
    dc = cp[None, :] - c[:, None] + (NA_COLS - 1)
    valid = (cp[None, :] >= cstart[:, None]) & (cp[None, :] < cstart[:, None] + NA_COLS)
    dcc = np.clip(dc, 0, 2 * NA_COLS - 2)
    b = rpb_l.astype(F32)[:, dr[:, :, None, None], dcc[None, None, :, :]]
    b = jnp.where(valid[None, None, None], b, MASK_VALUE)
    return b.transpose(0, 1, 3, 2, 4).reshape(NA_HEADS, NA_ROWS, GRID_W, NA_KEYS)


MLA_K_TILES = tuple(range(MLA_HEADS))
MLA_V_ROWS = tuple(range(MLA_HEADS))
GQA_K_TILES = (0,) * GQA_HEADS
GQA_V_ROWS = tuple(h // GQA_GROUP for h in range(GQA_HEADS))


def _trunk(x, layers, biases, tabs, g_final):
    n_layers = len(layers)
    for l, lw in enumerate(layers):
        na, mq, mk, mvt, gq, gk, gvt = _pre_call(x, lw, tabs)
        oa = _na_call(na, biases[l])
        ob = _attn_call(mq, mk, mvt, k_tiles=MLA_K_TILES, v_rows=MLA_V_ROWS, name="mla")
        oc = _attn_call(gq, gk, gvt, k_tiles=GQA_K_TILES, v_rows=GQA_V_ROWS, name="gqa")
        x = _post_call(x, oa, ob, oc, lw, g_final, final=(l == n_layers - 1))
    return x


def kernel(x_prompt, x_sample, attn_norm, w_in, q_a_norm, kv_a_norm, w_uq, w_ukv, q_norm_c, k_norm_c, rpb, g_out_a, g_out_b, g_out_c, w_out, ffn_norm, w_gate, w_up, w_down, final_norm):
    depth = w_in.shape[0]
    layers = [_layer_weights(l, attn_norm, w_in, q_a_norm, kv_a_norm, w_uq, w_ukv, q_norm_c,
                             k_norm_c, g_out_a, g_out_b, g_out_c, w_out, ffn_norm, w_gate,
                             w_up, w_down) for l in range(depth)]
    biases = [_na_bias_table(rpb[l]) for l in range(depth)]
    g_final = final_norm.reshape(1, -1).astype(F32)
    outs = []
    for x in (x_prompt, x_sample):
        tabs = _rope_tables(x.shape[1])
        outs.append(_trunk(x, layers, biases, tabs, g_final))
    return tuple(outs)
```

```python
import functools
import math

import jax
import jax.numpy as jnp
import numpy as np
from jax import lax
from jax.experimental import pallas as pl
from jax.experimental.pallas import tpu as pltpu

D_MODEL = 1024
GRID_W = 64
HEAD_DIM = 64
NA_HEADS = 4
NA_ROWS = 8
NA_COLS = 16
NA_D = NA_HEADS * HEAD_DIM
MLA_HEADS = 6
MLA_Q_RANK = 384
MLA_KV_RANK = 256
MLA_NOPE = 64
MLA_ROPE = 32
MLA_V = 64
MLA_QK = MLA_NOPE + MLA_ROPE
GQA_HEADS = 6
GQA_KV_HEADS = 2
GQA_GROUP = GQA_HEADS // GQA_KV_HEADS
D_FF = 2816
ROPE_BASE = 10000.0
EPS = 1e-6

LANES = 128
SUBLANES = 8
VMEM_LIMIT_BYTES = 56 * 1024 * 1024

PRE_TM = 512
ATTN_TQ = 512
ATTN_TK = PRE_TM
ONES_ROWS = 16
V_ROWS = HEAD_DIM + ONES_ROWS
NA_RB = 8
NA_UNROLL = 4
NA_KEYS = NA_ROWS * GRID_W
POST_TM = 256

LOG2E = 1.4426950408889634
MLA_QSCALE = LOG2E / math.sqrt(MLA_QK)
GQA_QSCALE = LOG2E / math.sqrt(HEAD_DIM)
NA_QSCALE = 1.0 / math.sqrt(HEAD_DIM)
MASK_VALUE = -1e30

BF16 = jnp.bfloat16
F32 = jnp.float32
NT_DIMS = (((1,), (1,)), ((), ()))


def _dot(a, b):
    return jnp.dot(a, b, preferred_element_type=F32)


def _dot_nt(a, b):
    return lax.dot_general(a, b, NT_DIMS, preferred_element_type=F32)


def _rms(x, g):
    return x * lax.rsqrt(jnp.mean(x * x, axis=-1, keepdims=True) + EPS) * g


def _with_ones_rows(vt, n_heads):
    ones = jnp.ones((ONES_ROWS, vt.shape[1]), F32)
    parts = []
    for h in range(n_heads):
        parts += [vt[h * HEAD_DIM:(h + 1) * HEAD_DIM], ones]
    return jnp.concatenate(parts, axis=0)


def _compiler_params(n_axes):
    return pltpu.CompilerParams(
        dimension_semantics=("arbitrary",) * n_axes,
        vmem_limit_bytes=VMEM_LIMIT_BYTES,
    )


def _const_spec(shape):
    nd = len(shape)
    return pl.BlockSpec(shape, lambda *_: (0,) * nd, pipeline_mode=pl.Buffered(1))


def _pre_kernel(x_ref, gattn_ref, wna_ref, wmla_ref, wqc_ref, wkc_ref, wvct_ref,
                qan_ref, kvan_ref, wuq_ref, wuk_ref, wuvt_ref, gq_ref, gk_ref,
                cosm_ref, sinm_ref, cosa_ref, sina_ref,
                na_ref, mq_ref, mk_ref, mvt_ref, gqo_ref, gko_ref, gvt_ref):
    x = x_ref[0]
    xb = _rms(x, gattn_ref[...]).astype(BF16)

    na_ref[0] = _dot(xb, wna_ref[...]).astype(BF16)

    cosm = cosm_ref[...]
    sinm = sinm_ref[...]
    cosa = cosa_ref[...]
    sina = sina_ref[...]

    lat = _dot(xb, wmla_ref[...])
    cq = lat[:, :MLA_Q_RANK]
    ckv = lat[:, MLA_Q_RANK:MLA_Q_RANK + MLA_KV_RANK]
    o = MLA_Q_RANK + MLA_KV_RANK
    kpe = lat[:, o:o + LANES] * cosm + lat[:, o + LANES:o + 2 * LANES] * sinm

    cqn = _rms(cq, qan_ref[...]).astype(BF16)
    q2 = _dot(cqn, wuq_ref[...])
    hw = MLA_HEADS * LANES
    for h in range(MLA_HEADS):
        sl = slice(h * LANES, (h + 1) * LANES)
        qh = q2[:, sl] * cosm + q2[:, hw + h * LANES:hw + (h + 1) * LANES] * sinm
        mq_ref[0, :, sl] = (qh * MLA_QSCALE).astype(BF16)

    ckvn = _rms(ckv, kvan_ref[...]).astype(BF16)
    kn = _dot(ckvn, wuk_ref[...])
    for h in range(MLA_HEADS):
        sl = slice(h * LANES, (h + 1) * LANES)
        mk_ref[0, :, sl] = (kn[:, sl] + kpe).astype(BF16)
    mvt_ref[0, 0] = _with_ones_rows(_dot_nt(wuvt_ref[...], ckvn), MLA_HEADS).astype(BF16)

    gq = gq_ref[...]
    gqc = cosa * gq[0:1]
    gqs = sina * gq[1:2]
    qc2 = _dot(xb, wqc_ref[...])
    hw = GQA_HEADS * LANES
    for h in range(GQA_HEADS):
        sl = slice(h * LANES, (h + 1) * LANES)
        xh = qc2[:, sl]
        xr = qc2[:, hw + h * LANES:hw + (h + 1) * LANES]
        ms = jnp.sum(xh * xh, axis=-1, keepdims=True) * (1.0 / HEAD_DIM)
        sc = lax.rsqrt(ms + EPS) * GQA_QSCALE
        gqo_ref[0, :, sl] = ((xh * gqc + xr * gqs) * sc).astype(BF16)

    gk = gk_ref[...]
    kc2 = _dot(xb, wkc_ref[...])
    xk = kc2[:, :LANES]
    xkr = kc2[:, LANES:]
    lane = lax.broadcasted_iota(jnp.int32, xk.shape, 1)
    first = lane < HEAD_DIM
    sq = xk * xk
    ms0 = jnp.sum(jnp.where(first, sq, 0.0), axis=-1, keepdims=True) * (1.0 / HEAD_DIM)
    ms1 = jnp.sum(jnp.where(first, 0.0, sq), axis=-1, keepdims=True) * (1.0 / HEAD_DIM)
    sc = jnp.where(first, lax.rsqrt(ms0 + EPS), lax.rsqrt(ms1 + EPS))
    gko_ref[0] = ((xk * (cosa * gk[0:1]) + xkr * (sina * gk[1:2])) * sc).astype(BF16)
    gvt_ref[0, 0] = _with_ones_rows(_dot_nt(wvct_ref[...], xb), GQA_KV_HEADS).astype(BF16)


def _pre_call(x, lw, tabs):
    B, S, _ = x.shape
    tm = PRE_TM
    nt = S // tm
    tile = lambda w: pl.BlockSpec((1, tm, w), lambda b, i: (b, i, 0))
    tab = pl.BlockSpec((tm, LANES), lambda b, i: (i, 0))
    weights = [lw["g_attn"], lw["w_na"], lw["w_mla"], lw["w_qc"], lw["w_kc"], lw["w_vct"],
               lw["g_qa"], lw["g_kva"], lw["w_uq"], lw["w_uk"], lw["w_uvt"], lw["g_q"], lw["g_k"]]
    out_shape = (
        jax.ShapeDtypeStruct((B, S, 3 * NA_D), BF16),
        jax.ShapeDtypeStruct((B, S, MLA_HEADS * LANES), BF16),
        jax.ShapeDtypeStruct((B, S, MLA_HEADS * LANES), BF16),
        jax.ShapeDtypeStruct((B, nt, MLA_HEADS * V_ROWS, tm), BF16),
        jax.ShapeDtypeStruct((B, S, GQA_HEADS * LANES), BF16),
        jax.ShapeDtypeStruct((B, S, GQA_KV_HEADS * HEAD_DIM), BF16),
        jax.ShapeDtypeStruct((B, nt, GQA_KV_HEADS * V_ROWS, tm), BF16),
    )
    out_specs = (
        tile(3 * NA_D), tile(MLA_HEADS * LANES), tile(MLA_HEADS * LANES),
        pl.BlockSpec((1, 1, MLA_HEADS * V_ROWS, tm), lambda b, i: (b, i, 0, 0)),
        tile(GQA_HEADS * LANES), tile(GQA_KV_HEADS * HEAD_DIM),
        pl.BlockSpec((1, 1, GQA_KV_HEADS * V_ROWS, tm), lambda b, i: (b, i, 0, 0)),
    )
    return pl.pallas_call(
        _pre_kernel,
        grid=(B, nt),
        in_specs=[tile(D_MODEL)] + [_const_spec(w.shape) for w in weights] + [tab] * 4,
        out_specs=out_specs,
        out_shape=out_shape,
        compiler_params=_compiler_params(2),
        name="pre",
    )(x, *weights, tabs["cos_m"], tabs["sin_m"], tabs["cos_a"], tabs["sin_a"])


def _column_max(x):
    rows = x.shape[0]
    while rows > SUBLANES:
        rows //= 2
        x = jnp.maximum(x[:rows], x[rows:])
    return jnp.max(x, axis=0, keepdims=True)


def _attn_kernel(q_ref, k_ref, vt_ref, o_ref, m_ref, acc_ref, *, n_heads, k_tiles, v_rows,
                 n_chunks, tk):
    m_ref[...] = jnp.full(m_ref.shape, -jnp.inf, F32)
    acc_ref[...] = jnp.zeros(acc_ref.shape, F32)

    def body(j, carry):
        start = pl.multiple_of(j * tk, tk)
        for h in range(n_heads):
            kt = k_tiles[h]
            vr = v_rows[h]
            qh = q_ref[0, :, h * LANES:(h + 1) * LANES]
            kc = k_ref[0, pl.ds(start, tk), kt * LANES:(kt + 1) * LANES]
            st = _dot_nt(kc, qh)
            m_old = m_ref[h]
            m_new = jnp.maximum(m_old, _column_max(st))
            alpha = jnp.exp2(m_old - m_new)
            p = jnp.exp2(st - m_new).astype(BF16)
            vc = vt_ref[0, j, vr * V_ROWS:(vr + 1) * V_ROWS, :]
            acc_ref[h] = alpha * acc_ref[h] + _dot(vc, p)
            m_ref[h] = m_new
        return carry

    lax.fori_loop(0, n_chunks, body, 0)
    outs = []
    for h in range(n_heads):
        acc = acc_ref[h]
        outs.append(acc[:HEAD_DIM] / acc[HEAD_DIM:HEAD_DIM + 1])
    o_ref[0] = jnp.concatenate(outs, axis=0).T


def _attn_call(q, k, vt, *, k_tiles, v_rows, name):
    B, S, qw = q.shape
    n_heads = qw // LANES
    tq, tk = ATTN_TQ, ATTN_TK
    n_chunks = S // tk
    kw = k.shape[2]
    vw = vt.shape[2]
    kern = functools.partial(_attn_kernel, n_heads=n_heads, k_tiles=k_tiles, v_rows=v_rows,
                             n_chunks=n_chunks, tk=tk)
    return pl.pallas_call(
        kern,
        grid=(B, S // tq),
        in_specs=[
            pl.BlockSpec((1, tq, qw), lambda b, i: (b, i, 0)),
            pl.BlockSpec((1, S, kw), lambda b, i: (b, 0, 0), pipeline_mode=pl.Buffered(1)),
            pl.BlockSpec((1, n_chunks, vw, tk), lambda b, i: (b, 0, 0, 0),
                         pipeline_mode=pl.Buffered(1)),
        ],
        out_specs=pl.BlockSpec((1, tq, n_heads * HEAD_DIM), lambda b, i: (b, i, 0)),
        out_shape=jax.ShapeDtypeStruct((B, S, n_heads * HEAD_DIM), F32),
        scratch_shapes=[pltpu.VMEM((n_heads, 1, tq), F32),
                        pltpu.VMEM((n_heads, V_ROWS, tq), F32)],
        compiler_params=_compiler_params(2),
        name=name,
    )(q, k, vt)


def _na_kernel(na_ref, bias_ref, o_ref, *, rows, rb):
    i = pl.program_id(1)
    lane = lax.broadcasted_iota(jnp.int32, (GRID_W, LANES), 1)
    first = lane < HEAD_DIM

    def row_body(rr, carry):
        r = i * rb + rr
        rs = jnp.clip(r - NA_ROWS // 2, 0, rows - NA_ROWS)
        delta = r - rs
        q = na_ref[0, pl.ds(pl.multiple_of(r * GRID_W, GRID_W), GRID_W), 0:NA_D]
        kv0 = pl.multiple_of(rs * GRID_W, GRID_W)
        for t in range(NA_HEADS // 2):
            qt = q[:, t * LANES:(t + 1) * LANES]
            kt = na_ref[0, pl.ds(kv0, NA_KEYS), NA_D + t * LANES:NA_D + (t + 1) * LANES]
            vt = na_ref[0, pl.ds(kv0, NA_KEYS), 2 * NA_D + t * LANES:2 * NA_D + (t + 1) * LANES]
            zero = jnp.zeros_like(qt)
            qm = jnp.concatenate([jnp.where(first, qt, zero), jnp.where(first, zero, qt)], axis=0)
            s = _dot_nt(qm, kt) + bias_ref[t, delta]
            m = jnp.max(s, axis=-1, keepdims=True)
            p = jnp.exp(s - m)
            l = jnp.sum(p, axis=-1, keepdims=True)
            o2 = _dot(p.astype(BF16), vt) / l
            o_ref[0, pl.ds(pl.multiple_of(rr * GRID_W, GRID_W), GRID_W),
                  t * LANES:(t + 1) * LANES] = jnp.where(first, o2[:GRID_W], o2[GRID_W:])
        return carry

    lax.fori_loop(0, rb, row_body, 0, unroll=NA_UNROLL)


def _na_call(na, bias):
    B, S, w = na.shape
    rows = S // GRID_W
    rb = NA_RB
    kern = functools.partial(_na_kernel, rows=rows, rb=rb)
    return pl.pallas_call(
        kern,
        grid=(B, rows // rb),
        in_specs=[
            pl.BlockSpec((1, S, w), lambda b, i: (b, 0, 0)),
            _const_spec(bias.shape),
        ],
        out_specs=pl.BlockSpec((1, rb * GRID_W, NA_D), lambda b, i: (b, i, 0)),
        out_shape=jax.ShapeDtypeStruct((B, S, NA_D), F32),
        compiler_params=_compiler_params(2),
        name="na",
    )(na, bias)


def _post_kernel(x_ref, oa_ref, ob_ref, oc_ref, ga_ref, gb_ref, gc_ref, wout_ref, gffn_ref,
                 wg_ref, wu_ref, wd_ref, gfin_ref, o_ref, *, final):
    merged = jnp.concatenate(
        [_rms(oa_ref[...], ga_ref[...]), _rms(ob_ref[...], gb_ref[...]),
         _rms(oc_ref[...], gc_ref[...])], axis=-1).astype(BF16)
    x1 = x_ref[...] + _dot(merged, wout_ref[...])
    hb = _rms(x1, gffn_ref[...]).astype(BF16)
    g = _dot(hb, wg_ref[...])
    u = _dot(hb, wu_ref[...])
    act = (g * jax.nn.sigmoid(g) * u).astype(BF16)
    x2 = x1 + _dot(act, wd_ref[...])
    if final:
        x2 = _rms(x2, gfin_ref[...])
    o_ref[...] = x2


def _post_call(x, oa, ob, oc, lw, g_final, final):
    B, S, _ = x.shape
    T = B * S
    tm = POST_TM
    flat = lambda a: a.reshape(T, a.shape[-1])
    tile = lambda w: pl.BlockSpec((tm, w), lambda i: (i, 0))
    weights_a = [lw["g_a"], lw["g_b"], lw["g_c"], lw["w_out"], lw["g_ffn"],
                 lw["w_gate"], lw["w_up"], lw["w_down"], g_final]
    out = pl.pallas_call(
        functools.partial(_post_kernel, final=final),
        grid=(T // tm,),
        in_specs=[tile(D_MODEL), tile(NA_D), tile(MLA_HEADS * MLA_V), tile(GQA_HEADS * HEAD_DIM)]
        + [_const_spec(w.shape) for w in weights_a],
        out_specs=tile(D_MODEL),
        out_shape=jax.ShapeDtypeStruct((T, D_MODEL), F32),
        compiler_params=_compiler_params(1),
        name="post",
    )(flat(x), flat(oa), flat(ob), flat(oc), *weights_a)
    return out.reshape(B, S, D_MODEL)


def _rot32(w):
    return jnp.concatenate([-w[:, 16:32], w[:, 0:16]], axis=1)


def _rot64(w):
    return jnp.concatenate([_rot32(w[:, :32]), _rot32(w[:, 32:])], axis=1)


def _perm64(g):
    p = lambda v: jnp.concatenate([v[16:32], v[0:16]])
    return jnp.concatenate([p(g[:32]), p(g[32:])])


def _layer_weights(l, attn_norm, w_in, q_a_norm, kv_a_norm, w_uq, w_ukv, q_norm_c, k_norm_c,
                   g_out_a, g_out_b, g_out_c, w_out, ffn_norm, w_gate, w_up, w_down):
    wi = w_in[l]
    K = wi.shape[0]
    sizes = (NA_D, NA_D, NA_D, MLA_Q_RANK, MLA_KV_RANK, MLA_ROPE,
             GQA_HEADS * HEAD_DIM, GQA_KV_HEADS * HEAD_DIM, GQA_KV_HEADS * HEAD_DIM)
    offs = np.concatenate([[0], np.cumsum(sizes)])
    (wqa, wka, wva, wcq, wckv, wkpe, wqc, wkc, wvc) = [wi[:, offs[i]:offs[i + 1]] for i in range(9)]
    z = lambda n: jnp.zeros((K, n), F32)

    w_na = jnp.concatenate([wqa * NA_QSCALE, wka, wva], axis=1)
    kpe_tile = jnp.concatenate([z(MLA_NOPE), wkpe, z(LANES - MLA_QK)], axis=1)
    kpe_rot = jnp.concatenate([z(MLA_NOPE), _rot32(wkpe), z(LANES - MLA_QK)], axis=1)
    w_mla = jnp.concatenate([wcq, wckv, kpe_tile, kpe_rot], axis=1)

    qc_tiles, qc_rots = [], []
    for h in range(GQA_HEADS):
        wh = wqc[:, h * HEAD_DIM:(h + 1) * HEAD_DIM]
        pad = (lambda a: jnp.concatenate([a, z(HEAD_DIM)], axis=1)) if h // GQA_GROUP == 0 else \
              (lambda a: jnp.concatenate([z(HEAD_DIM), a], axis=1))
        qc_tiles.append(pad(wh))
        qc_rots.append(pad(_rot64(wh)))
    w_qc = jnp.concatenate(qc_tiles + qc_rots, axis=1)
    w_kc = jnp.concatenate(
        [wkc] + [_rot64(wkc[:, h * HEAD_DIM:(h + 1) * HEAD_DIM]) for h in range(GQA_KV_HEADS)], axis=1)

    uq = w_uq[l]
    zq = lambda n: jnp.zeros((MLA_Q_RANK, n), F32)
    uq_tiles, uq_rots = [], []
    for h in range(MLA_HEADS):
        wh = uq[:, h * MLA_QK:(h + 1) * MLA_QK]
        uq_tiles.append(jnp.concatenate([wh, zq(LANES - MLA_QK)], axis=1))
        uq_rots.append(jnp.concatenate([zq(MLA_NOPE), _rot32(wh[:, MLA_NOPE:]), zq(LANES - MLA_QK)], axis=1))
    w_uq_ext = jnp.concatenate(uq_tiles + uq_rots, axis=1)

    ukv = w_ukv[l]
    zk = jnp.zeros((MLA_KV_RANK, LANES - MLA_NOPE), F32)
    per = MLA_NOPE + MLA_V
    w_uk = jnp.concatenate(
        [jnp.concatenate([ukv[:, h * per:h * per + MLA_NOPE], zk], axis=1) for h in range(MLA_HEADS)], axis=1)
    w_uvt = jnp.concatenate([ukv[:, h * per + MLA_NOPE:(h + 1) * per] for h in range(MLA_HEADS)], axis=1).T

    two = lambda g: jnp.concatenate([g, g])
    row = lambda g: g.reshape(1, -1).astype(F32)
    return {
        "g_attn": row(attn_norm[l]),
        "w_na": w_na.astype(BF16), "w_mla": w_mla.astype(BF16), "w_qc": w_qc.astype(BF16),
        "w_kc": w_kc.astype(BF16), "w_vct": wvc.T.astype(BF16),
        "g_qa": row(q_a_norm[l]), "g_kva": row(kv_a_norm[l]),
        "w_uq": w_uq_ext.astype(BF16), "w_uk": w_uk.astype(BF16), "w_uvt": w_uvt.astype(BF16),
        "g_q": jnp.stack([two(q_norm_c[l]), two(_perm64(q_norm_c[l]))]).astype(F32),
        "g_k": jnp.stack([two(k_norm_c[l]), two(_perm64(k_norm_c[l]))]).astype(F32),
        "g_a": row(g_out_a[l]), "g_b": row(g_out_b[l]), "g_c": row(g_out_c[l]),
        "w_out": w_out[l].astype(BF16), "g_ffn": row(ffn_norm[l]),
        "w_gate": w_gate[l].astype(BF16), "w_up": w_up[l].astype(BF16), "w_down": w_down[l].astype(BF16),
    }


def _rope_tables(S):
    tok = jnp.arange(S)
    t = tok.astype(F32)
    row = (tok // GRID_W).astype(F32)
    col = (tok % GRID_W).astype(F32)
    half = MLA_ROPE // 2
    inv = ROPE_BASE ** (-(jnp.arange(half, dtype=F32) * 2.0) / MLA_ROPE)

    def cs(pos):
        ang = pos[:, None] * inv[None, :]
        c, s = jnp.cos(ang), jnp.sin(ang)
        return jnp.concatenate([c, c], axis=1), jnp.concatenate([s, s], axis=1)

    ct, st = cs(t)
    ones = jnp.ones((S, MLA_NOPE), F32)
    zeros = jnp.zeros((S, MLA_NOPE), F32)
    pad1 = jnp.ones((S, LANES - MLA_QK), F32)
    pad0 = jnp.zeros((S, LANES - MLA_QK), F32)
    cr, sr = cs(row)
    cc, sc = cs(col)
    ca = jnp.concatenate([cr, cc], axis=1)
    sa = jnp.concatenate([sr, sc], axis=1)
    return {
        "cos_m": jnp.concatenate([ones, ct, pad1], axis=1),
        "sin_m": jnp.concatenate([zeros, st, pad0], axis=1),
        "cos_a": jnp.concatenate([ca, ca], axis=1),
        "sin_a": jnp.concatenate([sa, sa], axis=1),
    }


def _na_bias_table(rpb_l):
    c = np.arange(GRID_W)
    cstart = np.clip(c - NA_COLS // 2, 0, GRID_W - NA_COLS)
    cp = np.arange(GRID_W)
    dc = cp[None, :] - c[:, None] + (NA_COLS - 1)
    valid = (cp[None, :] >= cstart[:, None]) & (cp[None, :] < cstart[:, None] + NA_COLS)
    onehot = ((dc[:, :, None] == np.arange(2 * NA_COLS - 1)) & valid[:, :, None]).astype(np.float32)
    t = jnp.einsum("hrd,cpd->hrcp", rpb_l.astype(F32), onehot, precision=lax.Precision.HIGHEST)
    t = jnp.where(valid[None, None], t, MASK_VALUE)
    b = jnp.stack([t[:, NA_ROWS - 1 - d:2 * NA_ROWS - 1 - d] for d in range(NA_ROWS)], axis=1)
    b = b.transpose(0, 1, 3, 2, 4).reshape(NA_HEADS // 2, 2, NA_ROWS, GRID_W, NA_KEYS)
    return b.transpose(0, 2, 1, 3, 4).reshape(NA_HEADS // 2, NA_ROWS, 2 * GRID_W, NA_KEYS)


MLA_K_TILES = tuple(range(MLA_HEADS))
MLA_V_ROWS = tuple(range(MLA_HEADS))
GQA_K_TILES = (0,) * GQA_HEADS
GQA_V_ROWS = tuple(h // GQA_GROUP for h in range(GQA_HEADS))


def _trunk(x, layers, biases, tabs, g_final):
    n_layers = len(layers)
    for l, lw in enumerate(layers):
        na, mq, mk, mvt, gq, gk, gvt = _pre_call(x, lw, tabs)
        oa = _na_call(na, biases[l])
        ob = _attn_call(mq, mk, mvt, k_tiles=MLA_K_TILES, v_rows=MLA_V_ROWS, name="mla")
        oc = _attn_call(gq, gk, gvt, k_tiles=GQA_K_TILES, v_rows=GQA_V_ROWS, name="gqa")
        x = _post_call(x, oa, ob, oc, lw, g_final, final=(l == n_layers - 1))
    return x


def kernel(x_prompt, x_sample, attn_norm, w_in, q_a_norm, kv_a_norm, w_uq, w_ukv, q_norm_c, k_norm_c, rpb, g_out_a, g_out_b, g_out_c, w_out, ffn_norm, w_gate, w_up, w_down, final_norm):
    depth = w_in.shape[0]
    layers = [_layer_weights(l, attn_norm, w_in, q_a_norm, kv_a_norm, w_uq, w_ukv, q_norm_c,
                             k_norm_c, g_out_a, g_out_b, g_out_c, w_out, ffn_norm, w_gate,
                             w_up, w_down) for l in range(depth)]
    biases = [_na_bias_table(rpb[l]) for l in range(depth)]
    g_final = final_norm.reshape(1, -1).astype(F32)
    outs = []
    for x in (x_prompt, x_sample):
        tabs = _rope_tables(x.shape[1])
        outs.append(_trunk(x, layers, biases, tabs, g_final))
    return tuple(outs)
```

```python
import functools
import math

import jax
import jax.numpy as jnp
import numpy as np
from jax import lax
from jax.experimental import pallas as pl
from jax.experimental.pallas import tpu as pltpu

D_MODEL = 1024
GRID_W = 64
HEAD_DIM = 64
NA_HEADS = 4
NA_ROWS = 8
NA_COLS = 16
NA_D = NA_HEADS * HEAD_DIM
MLA_HEADS = 6
MLA_Q_RANK = 384
MLA_KV_RANK = 256
MLA_NOPE = 64
MLA_ROPE = 32
MLA_V = 64
MLA_QK = MLA_NOPE + MLA_ROPE
GQA_HEADS = 6
GQA_KV_HEADS = 2
GQA_GROUP = GQA_HEADS // GQA_KV_HEADS
D_FF = 2816
ROPE_BASE = 10000.0
EPS = 1e-6

LANES = 128
SUBLANES = 8
VMEM_LIMIT_BYTES = 56 * 1024 * 1024

PRE_TM = 512
ATTN_TQ = 512
ATTN_TK = PRE_TM
ONES_ROWS = 16
V_ROWS = HEAD_DIM + ONES_ROWS
NA_RB = 8
NA_UNROLL = 4
NA_KEYS = NA_ROWS * GRID_W
POST_TM = 256

LOG2E = 1.4426950408889634
MLA_QSCALE = LOG2E / math.sqrt(MLA_QK)
GQA_QSCALE = LOG2E / math.sqrt(HEAD_DIM)
NA_QSCALE = 1.0 / math.sqrt(HEAD_DIM)
MASK_VALUE = -1e30

BF16 = jnp.bfloat16
F32 = jnp.float32
NT_DIMS = (((1,), (1,)), ((), ()))


def _dot(a, b):
    return jnp.dot(a, b, preferred_element_type=F32)


def _dot_nt(a, b):
    return lax.dot_general(a, b, NT_DIMS, preferred_element_type=F32)


def _rms(x, g):
    return x * lax.rsqrt(jnp.mean(x * x, axis=-1, keepdims=True) + EPS) * g


def _with_ones_rows(vt, n_heads):
    ones = jnp.ones((ONES_ROWS, vt.shape[1]), F32)
    parts = []
    for h in range(n_heads):
        parts += [vt[h * HEAD_DIM:(h + 1) * HEAD_DIM], ones]
    return jnp.concatenate(parts, axis=0)


def _compiler_params(n_axes):
    return pltpu.CompilerParams(
        dimension_semantics=("arbitrary",) * n_axes,
        vmem_limit_bytes=VMEM_LIMIT_BYTES,
    )


def _const_spec(shape):
    nd = len(shape)
    return pl.BlockSpec(shape, lambda *_: (0,) * nd, pipeline_mode=pl.Buffered(1))


def _pre_kernel(x_ref, gattn_ref, wna_ref, wmla_ref, wqc_ref, wkc_ref, wvct_ref,
                qan_ref, kvan_ref, wuq_ref, wuk_ref, wuvt_ref, gq_ref, gk_ref,
                cosm_ref, sinm_ref, cosa_ref, sina_ref,
                na_ref, mq_ref, mk_ref, mvt_ref, gqo_ref, gko_ref, gvt_ref):
    x = x_ref[0]
    xb = _rms(x, gattn_ref[...]).astype(BF16)

    na_ref[0] = _dot(xb, wna_ref[...]).astype(BF16)

    cosm = cosm_ref[...]
    sinm = sinm_ref[...]
    cosa = cosa_ref[...]
    sina = sina_ref[...]

    lat = _dot(xb, wmla_ref[...])
    cq = lat[:, :MLA_Q_RANK]
    ckv = lat[:, MLA_Q_RANK:MLA_Q_RANK + MLA_KV_RANK]
    o = MLA_Q_RANK + MLA_KV_RANK
    kpe = lat[:, o:o + LANES] * cosm + lat[:, o + LANES:o + 2 * LANES] * sinm

    cqn = _rms(cq, qan_ref[...]).astype(BF16)
    q2 = _dot(cqn, wuq_ref[...])
    hw = MLA_HEADS * LANES
    for h in range(MLA_HEADS):
        sl = slice(h * LANES, (h + 1) * LANES)
        qh = q2[:, sl] * cosm + q2[:, hw + h * LANES:hw + (h + 1) * LANES] * sinm
        mq_ref[0, :, sl] = (qh * MLA_QSCALE).astype(BF16)

    ckvn = _rms(ckv, kvan_ref[...]).astype(BF16)
    kn = _dot(ckvn, wuk_ref[...])
    for h in range(MLA_HEADS):
        sl = slice(h * LANES, (h + 1) * LANES)
        mk_ref[0, :, sl] = (kn[:, sl] + kpe).astype(BF16)
    mvt_ref[0, 0] = _with_ones_rows(_dot_nt(wuvt_ref[...], ckvn), MLA_HEADS).astype(BF16)

    gq = gq_ref[...]
    gqc = cosa * gq[0:1]
    gqs = sina * gq[1:2]
    qc2 = _dot(xb, wqc_ref[...])
    hw = GQA_HEADS * LANES
    for h in range(GQA_HEADS):
        sl = slice(h * LANES, (h + 1) * LANES)
        xh = qc2[:, sl]
        xr = qc2[:, hw + h * LANES:hw + (h + 1) * LANES]
        ms = jnp.sum(xh * xh, axis=-1, keepdims=True) * (1.0 / HEAD_DIM)
        sc = lax.rsqrt(ms + EPS) * GQA_QSCALE
        gqo_ref[0, :, sl] = ((xh * gqc + xr * gqs) * sc).astype(BF16)

    gk = gk_ref[...]
    kc2 = _dot(xb, wkc_ref[...])
    xk = kc2[:, :LANES]
    xkr = kc2[:, LANES:]
    lane = lax.broadcasted_iota(jnp.int32, xk.shape, 1)
    first = lane < HEAD_DIM
    sq = xk * xk
    ms0 = jnp.sum(jnp.where(first, sq, 0.0), axis=-1, keepdims=True) * (1.0 / HEAD_DIM)
    ms1 = jnp.sum(jnp.where(first, 0.0, sq), axis=-1, keepdims=True) * (1.0 / HEAD_DIM)
    sc = jnp.where(first, lax.rsqrt(ms0 + EPS), lax.rsqrt(ms1 + EPS))
    gko_ref[0] = ((xk * (cosa * gk[0:1]) + xkr * (sina * gk[1:2])) * sc).astype(BF16)
    gvt_ref[0, 0] = _with_ones_rows(_dot_nt(wvct_ref[...], xb), GQA_KV_HEADS).astype(BF16)


def _pre_call(x, lw, tabs):
    B, S, _ = x.shape
    tm = PRE_TM
    nt = S // tm
    tile = lambda w: pl.BlockSpec((1, tm, w), lambda b, i: (b, i, 0))
    tab = pl.BlockSpec((tm, LANES), lambda b, i: (i, 0))
    weights = [lw["g_attn"], lw["w_na"], lw["w_mla"], lw["w_qc"], lw["w_kc"], lw["w_vct"],
               lw["g_qa"], lw["g_kva"], lw["w_uq"], lw["w_uk"], lw["w_uvt"], lw["g_q"], lw["g_k"]]
    out_shape = (
        jax.ShapeDtypeStruct((B, S, 3 * NA_D), BF16),
        jax.ShapeDtypeStruct((B, S, MLA_HEADS * LANES), BF16),
        jax.ShapeDtypeStruct((B, S, MLA_HEADS * LANES), BF16),
        jax.ShapeDtypeStruct((B, nt, MLA_HEADS * V_ROWS, tm), BF16),
        jax.ShapeDtypeStruct((B, S, GQA_HEADS * LANES), BF16),
        jax.ShapeDtypeStruct((B, S, GQA_KV_HEADS * HEAD_DIM), BF16),
        jax.ShapeDtypeStruct((B, nt, GQA_KV_HEADS * V_ROWS, tm), BF16),
    )
    out_specs = (
        tile(3 * NA_D), tile(MLA_HEADS * LANES), tile(MLA_HEADS * LANES),
        pl.BlockSpec((1, 1, MLA_HEADS * V_ROWS, tm), lambda b, i: (b, i, 0, 0)),
        tile(GQA_HEADS * LANES), tile(GQA_KV_HEADS * HEAD_DIM),
        pl.BlockSpec((1, 1, GQA_KV_HEADS * V_ROWS, tm), lambda b, i: (b, i, 0, 0)),
    )
    return pl.pallas_call(
        _pre_kernel,
        grid=(B, nt),
        in_specs=[tile(D_MODEL)] + [_const_spec(w.shape) for w in weights] + [tab] * 4,
        out_specs=out_specs,
        out_shape=out_shape,
        compiler_params=_compiler_params(2),
        name="pre",
    )(x, *weights, tabs["cos_m"], tabs["sin_m"], tabs["cos_a"], tabs["sin_a"])


def _column_max(x):
    rows = x.shape[0]
    while rows > SUBLANES:
        rows //= 2
        x = jnp.maximum(x[:rows], x[rows:])
    return jnp.max(x, axis=0, keepdims=True)


def _attn_kernel(q_ref, k_ref, vt_ref, o_ref, m_ref, acc_ref, s_ref, cm_ref, *, n_heads,
                 k_tiles, v_rows, n_chunks, tk):
    m_ref[...] = jnp.full(m_ref.shape, -jnp.inf, F32)
    acc_ref[...] = jnp.zeros(acc_ref.shape, F32)

    def scores(j, slot, h):
        start = pl.multiple_of(j * tk, tk)
        kt = k_tiles[h]
        qh = q_ref[0, :, h * LANES:(h + 1) * LANES]
        kc = k_ref[0, pl.ds(start, tk), kt * LANES:(kt + 1) * LANES]
        st = _dot_nt(kc, qh)
        s_ref[slot, h] = st
        cm_ref[slot, h] = _column_max(st)

    def accumulate(j, slot, h):
        vr = v_rows[h]
        m_old = m_ref[h]
        m_new = jnp.maximum(m_old, cm_ref[slot, h])
        alpha = jnp.exp2(m_old - m_new)
        p = jnp.exp2(s_ref[slot, h] - m_new).astype(BF16)
        vc = vt_ref[0, j, vr * V_ROWS:(vr + 1) * V_ROWS, :]
        acc_ref[h] = alpha * acc_ref[h] + _dot(vc, p)
        m_ref[h] = m_new

    def step(j, slot):
        for h in range(n_heads):
            accumulate(j, slot, h)
            scores(j + 1, 1 - slot, h)

    for h in range(n_heads):
        scores(0, 0, h)

    def body(jj, carry):
        step(2 * jj, 0)
        step(2 * jj + 1, 1)
        return carry

    lax.fori_loop(0, n_chunks // 2 - 1, body, 0)
    step(n_chunks - 2, 0)
    for h in range(n_heads):
        accumulate(n_chunks - 1, 1, h)
    outs = []
    for h in range(n_heads):
        acc = acc_ref[h]
        outs.append(acc[:HEAD_DIM] / acc[HEAD_DIM:HEAD_DIM + 1])
    o_ref[0] = jnp.concatenate(outs, axis=0).T


def _attn_call(q, k, vt, *, k_tiles, v_rows, name):
    B, S, qw = q.shape
    n_heads = qw // LANES
    tq, tk = ATTN_TQ, ATTN_TK
    n_chunks = S // tk
    assert S % tq == 0 and S % tk == 0 and n_chunks % 2 == 0, (S, tq, tk)
    kw = k.shape[2]
    vw = vt.shape[2]
    kern = functools.partial(_attn_kernel, n_heads=n_heads, k_tiles=k_tiles, v_rows=v_rows,
                             n_chunks=n_chunks, tk=tk)
    return pl.pallas_call(
        kern,
        grid=(B, S // tq),
        in_specs=[
            pl.BlockSpec((1, tq, qw), lambda b, i: (b, i, 0)),
            pl.BlockSpec((1, S, kw), lambda b, i: (b, 0, 0), pipeline_mode=pl.Buffered(1)),
            pl.BlockSpec((1, n_chunks, vw, tk), lambda b, i: (b, 0, 0, 0),
                         pipeline_mode=pl.Buffered(1)),
        ],
        out_specs=pl.BlockSpec((1, tq, n_heads * HEAD_DIM), lambda b, i: (b, i, 0)),
        out_shape=jax.ShapeDtypeStruct((B, S, n_heads * HEAD_DIM), F32),
        scratch_shapes=[pltpu.VMEM((n_heads, 1, tq), F32),
                        pltpu.VMEM((n_heads, V_ROWS, tq), F32),
                        pltpu.VMEM((2, n_heads, tk, tq), F32),
                        pltpu.VMEM((2, n_heads, 1, tq), F32)],
        compiler_params=_compiler_params(2),
        name=name,
    )(q, k, vt)


def _na_kernel(na_ref, bias_ref, o_ref, *, rows, rb):
    i = pl.program_id(1)
    lane = lax.broadcasted_iota(jnp.int32, (GRID_W, LANES), 1)
    first = lane < HEAD_DIM

    def row_body(rr, carry):
        r = i * rb + rr
        rs = jnp.clip(r - NA_ROWS // 2, 0, rows - NA_ROWS)
        delta = r - rs
        q = na_ref[0, pl.ds(pl.multiple_of(r * GRID_W, GRID_W), GRID_W), 0:NA_D]
        kv0 = pl.multiple_of(rs * GRID_W, GRID_W)
        for t in range(NA_HEADS // 2):
            qt = q[:, t * LANES:(t + 1) * LANES]
            kt = na_ref[0, pl.ds(kv0, NA_KEYS), NA_D + t * LANES:NA_D + (t + 1) * LANES]
            vt = na_ref[0, pl.ds(kv0, NA_KEYS), 2 * NA_D + t * LANES:2 * NA_D + (t + 1) * LANES]
            zero = jnp.zeros_like(qt)
            qm = jnp.concatenate([jnp.where(first, qt, zero), jnp.where(first, zero, qt)], axis=0)
            s = _dot_nt(qm, kt) + bias_ref[t, delta]
            m = jnp.max(s, axis=-1, keepdims=True)
            p = jnp.exp(s - m)
            l = jnp.sum(p, axis=-1, keepdims=True)
            o2 = _dot(p.astype(BF16), vt) / l
            o_ref[0, pl.ds(pl.multiple_of(rr * GRID_W, GRID_W), GRID_W),
                  t * LANES:(t + 1) * LANES] = jnp.where(first, o2[:GRID_W], o2[GRID_W:])
        return carry

    lax.fori_loop(0, rb, row_body, 0, unroll=NA_UNROLL)


def _na_call(na, bias):
    B, S, w = na.shape
    rows = S // GRID_W
    rb = NA_RB
    kern = functools.partial(_na_kernel, rows=rows, rb=rb)
    return pl.pallas_call(
        kern,
        grid=(B, rows // rb),
        in_specs=[
            pl.BlockSpec((1, S, w), lambda b, i: (b, 0, 0)),
            _const_spec(bias.shape),
        ],
        out_specs=pl.BlockSpec((1, rb * GRID_W, NA_D), lambda b, i: (b, i, 0)),
        out_shape=jax.ShapeDtypeStruct((B, S, NA_D), F32),
        compiler_params=_compiler_params(2),
        name="na",
    )(na, bias)


def _post_kernel(x_ref, oa_ref, ob_ref, oc_ref, ga_ref, gb_ref, gc_ref, wout_ref, gffn_ref,
                 wg_ref, wu_ref, wd_ref, gfin_ref, o_ref, *, final):
    merged = jnp.concatenate(
        [_rms(oa_ref[...], ga_ref[...]), _rms(ob_ref[...], gb_ref[...]),
         _rms(oc_ref[...], gc_ref[...])], axis=-1).astype(BF16)
    x1 = x_ref[...] + _dot(merged, wout_ref[...])
    hb = _rms(x1, gffn_ref[...]).astype(BF16)
    g = _dot(hb, wg_ref[...])
    u = _dot(hb, wu_ref[...])
    act = (g * jax.nn.sigmoid(g) * u).astype(BF16)
    x2 = x1 + _dot(act, wd_ref[...])
    if final:
        x2 = _rms(x2, gfin_ref[...])
    o_ref[...] = x2


def _post_call(x, oa, ob, oc, lw, g_final, final):
    B, S, _ = x.shape
    T = B * S
    tm = POST_TM
    flat = lambda a: a.reshape(T, a.shape[-1])
    tile = lambda w: pl.BlockSpec((tm, w), lambda i: (i, 0))
    weights_a = [lw["g_a"], lw["g_b"], lw["g_c"], lw["w_out"], lw["g_ffn"],
                 lw["w_gate"], lw["w_up"], lw["w_down"], g_final]
    out = pl.pallas_call(
        functools.partial(_post_kernel, final=final),
        grid=(T // tm,),
        in_specs=[tile(D_MODEL), tile(NA_D), tile(MLA_HEADS * MLA_V), tile(GQA_HEADS * HEAD_DIM)]
        + [_const_spec(w.shape) for w in weights_a],
        out_specs=tile(D_MODEL),
        out_shape=jax.ShapeDtypeStruct((T, D_MODEL), F32),
        compiler_params=_compiler_params(1),
        name="post",
    )(flat(x), flat(oa), flat(ob), flat(oc), *weights_a)
    return out.reshape(B, S, D_MODEL)


def _rot32(w):
    return jnp.concatenate([-w[:, 16:32], w[:, 0:16]], axis=1)


def _rot64(w):
    return jnp.concatenate([_rot32(w[:, :32]), _rot32(w[:, 32:])], axis=1)


def _perm64(g):
    p = lambda v: jnp.concatenate([v[16:32], v[0:16]])
    return jnp.concatenate([p(g[:32]), p(g[32:])])


def _layer_weights(l, attn_norm, w_in, q_a_norm, kv_a_norm, w_uq, w_ukv, q_norm_c, k_norm_c,
                   g_out_a, g_out_b, g_out_c, w_out, ffn_norm, w_gate, w_up, w_down):
    wi = w_in[l]
    K = wi.shape[0]
    sizes = (NA_D, NA_D, NA_D, MLA_Q_RANK, MLA_KV_RANK, MLA_ROPE,
             GQA_HEADS * HEAD_DIM, GQA_KV_HEADS * HEAD_DIM, GQA_KV_HEADS * HEAD_DIM)
    offs = np.concatenate([[0], np.cumsum(sizes)])
    (wqa, wka, wva, wcq, wckv, wkpe, wqc, wkc, wvc) = [wi[:, offs[i]:offs[i + 1]] for i in range(9)]
    z = lambda n: jnp.zeros((K, n), F32)

    w_na = jnp.concatenate([wqa * NA_QSCALE, wka, wva], axis=1)
    kpe_tile = jnp.concatenate([z(MLA_NOPE), wkpe, z(LANES - MLA_QK)], axis=1)
    kpe_rot = jnp.concatenate([z(MLA_NOPE), _rot32(wkpe), z(LANES - MLA_QK)], axis=1)
    w_mla = jnp.concatenate([wcq, wckv, kpe_tile, kpe_rot], axis=1)

    qc_tiles, qc_rots = [], []
    for h in range(GQA_HEADS):
        wh = wqc[:, h * HEAD_DIM:(h + 1) * HEAD_DIM]
        pad = (lambda a: jnp.concatenate([a, z(HEAD_DIM)], axis=1)) if h // GQA_GROUP == 0 else \
              (lambda a: jnp.concatenate([z(HEAD_DIM), a], axis=1))
        qc_tiles.append(pad(wh))
        qc_rots.append(pad(_rot64(wh)))
    w_qc = jnp.concatenate(qc_tiles + qc_rots, axis=1)
    w_kc = jnp.concatenate(
        [wkc] + [_rot64(wkc[:, h * HEAD_DIM:(h + 1) * HEAD_DIM]) for h in range(GQA_KV_HEADS)], axis=1)

    uq = w_uq[l]
    zq = lambda n: jnp.zeros((MLA_Q_RANK, n), F32)
    uq_tiles, uq_rots = [], []
    for h in range(MLA_HEADS):
        wh = uq[:, h * MLA_QK:(h + 1) * MLA_QK]
        uq_tiles.append(jnp.concatenate([wh, zq(LANES - MLA_QK)], axis=1))
        uq_rots.append(jnp.concatenate([zq(MLA_NOPE), _rot32(wh[:, MLA_NOPE:]), zq(LANES - MLA_QK)], axis=1))
    w_uq_ext = jnp.concatenate(uq_tiles + uq_rots, axis=1)

    ukv = w_ukv[l]
    zk = jnp.zeros((MLA_KV_RANK, LANES - MLA_NOPE), F32)
    per = MLA_NOPE + MLA_V
    w_uk = jnp.concatenate(
        [jnp.concatenate([ukv[:, h * per:h * per + MLA_NOPE], zk], axis=1) for h in range(MLA_HEADS)], axis=1)
    w_uvt = jnp.concatenate([ukv[:, h * per + MLA_NOPE:(h + 1) * per] for h in range(MLA_HEADS)], axis=1).T

    two = lambda g: jnp.concatenate([g, g])
    row = lambda g: g.reshape(1, -1).astype(F32)
    return {
        "g_attn": row(attn_norm[l]),
        "w_na": w_na.astype(BF16), "w_mla": w_mla.astype(BF16), "w_qc": w_qc.astype(BF16),
        "w_kc": w_kc.astype(BF16), "w_vct": wvc.T.astype(BF16),
        "g_qa": row(q_a_norm[l]), "g_kva": row(kv_a_norm[l]),
        "w_uq": w_uq_ext.astype(BF16), "w_uk": w_uk.astype(BF16), "w_uvt": w_uvt.astype(BF16),
        "g_q": jnp.stack([two(q_norm_c[l]), two(_perm64(q_norm_c[l]))]).astype(F32),
        "g_k": jnp.stack([two(k_norm_c[l]), two(_perm64(k_norm_c[l]))]).astype(F32),
        "g_a": row(g_out_a[l]), "g_b": row(g_out_b[l]), "g_c": row(g_out_c[l]),
        "w_out": w_out[l].astype(BF16), "g_ffn": row(ffn_norm[l]),
        "w_gate": w_gate[l].astype(BF16), "w_up": w_up[l].astype(BF16), "w_down": w_down[l].astype(BF16),
    }


def _rope_tables(S):
    tok = jnp.arange(S)
    t = tok.astype(F32)
    row = (tok // GRID_W).astype(F32)
    col = (tok % GRID_W).astype(F32)
    half = MLA_ROPE // 2
    inv = ROPE_BASE ** (-(jnp.arange(half, dtype=F32) * 2.0) / MLA_ROPE)

    def cs(pos):
        ang = pos[:, None] * inv[None, :]
        c, s = jnp.cos(ang), jnp.sin(ang)
        return jnp.concatenate([c, c], axis=1), jnp.concatenate([s, s], axis=1)

    ct, st = cs(t)
    ones = jnp.ones((S, MLA_NOPE), F32)
    zeros = jnp.zeros((S, MLA_NOPE), F32)
    pad1 = jnp.ones((S, LANES - MLA_QK), F32)
    pad0 = jnp.zeros((S, LANES - MLA_QK), F32)
    cr, sr = cs(row)
    cc, sc = cs(col)
    ca = jnp.concatenate([cr, cc], axis=1)
    sa = jnp.concatenate([sr, sc], axis=1)
    return {
        "cos_m": jnp.concatenate([ones, ct, pad1], axis=1),
        "sin_m": jnp.concatenate([zeros, st, pad0], axis=1),
        "cos_a": jnp.concatenate([ca, ca], axis=1),
        "sin_a": jnp.concatenate([sa, sa], axis=1),
    }


def _na_bias_table(rpb_l):
    c = np.arange(GRID_W)
    cstart = np.clip(c - NA_COLS // 2, 0, GRID_W - NA_COLS)
    cp = np.arange(GRID_W)
    dc = cp[None, :] - c[:, None] + (NA_COLS - 1)
    valid = (cp[None, :] >= cstart[:, None]) & (cp[None, :] < cstart[:, None] + NA_COLS)
    onehot = ((dc[:, :, None] == np.arange(2 * NA_COLS - 1)) & valid[:, :, None]).astype(np.float32)
    t = jnp.einsum("hrd,cpd->hrcp", rpb_l.astype(F32), onehot, precision=lax.Precision.HIGHEST)
    t = jnp.where(valid[None, None], t, MASK_VALUE)
    b = jnp.stack([t[:, NA_ROWS - 1 - d:2 * NA_ROWS - 1 - d] for d in range(NA_ROWS)], axis=1)
    b = b.transpose(0, 1, 3, 2, 4).reshape(NA_HEADS // 2, 2, NA_ROWS, GRID_W, NA_KEYS)
    return b.transpose(0, 2, 1, 3, 4).reshape(NA_HEADS // 2, NA_ROWS, 2 * GRID_W, NA_KEYS)


MLA_K_TILES = tuple(range(MLA_HEADS))
MLA_V_ROWS = tuple(range(MLA_HEADS))
GQA_K_TILES = (0,) * GQA_HEADS
GQA_V_ROWS = tuple(h // GQA_GROUP for h in range(GQA_HEADS))


def _trunk(x, layers, biases, tabs, g_final):
    n_layers = len(layers)
    for l, lw in enumerate(layers):
        na, mq, mk, mvt, gq, gk, gvt = _pre_call(x, lw, tabs)
        oa = _na_call(na, biases[l])
        ob = _attn_call(mq, mk, mvt, k_tiles=MLA_K_TILES, v_rows=MLA_V_ROWS, name="mla")
        oc = _attn_call(gq, gk, gvt, k_tiles=GQA_K_TILES, v_rows=GQA_V_ROWS, name="gqa")
        x = _post_call(x, oa, ob, oc, lw, g_final, final=(l == n_layers - 1))
    return x


def kernel(x_prompt, x_sample, attn_norm, w_in, q_a_norm, kv_a_norm, w_uq, w_ukv, q_norm_c, k_norm_c, rpb, g_out_a, g_out_b, g_out_c, w_out, ffn_norm, w_gate, w_up, w_down, final_norm):
    depth = w_in.shape[0]
    layers = [_layer_weights(l, attn_norm, w_in, q_a_norm, kv_a_norm, w_uq, w_ukv, q_norm_c,
                             k_norm_c, g_out_a, g_out_b, g_out_c, w_out, ffn_norm, w_gate,
                             w_up, w_down) for l in range(depth)]
    biases = [_na_bias_table(rpb[l]) for l in range(depth)]
    g_final = final_norm.reshape(1, -1).astype(F32)
    outs = []
    for x in (x_prompt, x_sample):
        tabs = _rope_tables(x.shape[1])
        outs.append(_trunk(x, layers, biases, tabs, g_final))
    return tuple(outs)
```

```python
import functools
import math

import jax
import jax.numpy as jnp
import numpy as np
from jax import lax
from jax.experimental import pallas as pl
from jax.experimental.pallas import tpu as pltpu

D_MODEL = 1024
GRID_W = 64
HEAD_DIM = 64
NA_HEADS = 4
NA_ROWS = 8
NA_COLS = 16
NA_D = NA_HEADS * HEAD_DIM
MLA_HEADS = 6
MLA_Q_RANK = 384
MLA_KV_RANK = 256
MLA_NOPE = 64
MLA_ROPE = 32
MLA_V = 64
MLA_QK = MLA_NOPE + MLA_ROPE
GQA_HEADS = 6
GQA_KV_HEADS = 2
GQA_GROUP = GQA_HEADS // GQA_KV_HEADS
D_FF = 2816
ROPE_BASE = 10000.0
EPS = 1e-6

LANES = 128
SUBLANES = 8
VMEM_LIMIT_BYTES = 56 * 1024 * 1024
ATTN_VMEM_BUDGET_BYTES = 48 * 1024 * 1024

PRE_TM = 512
ATTN_TQ = 512
ATTN_TK = PRE_TM
ATTN_TILES = 4
ONES_ROWS = 16
V_ROWS = HEAD_DIM + ONES_ROWS
NA_RB = 8
NA_KEYS = NA_ROWS * GRID_W
POST_TM = 256

LOG2E = 1.4426950408889634
MLA_QSCALE = LOG2E / math.sqrt(MLA_QK)
GQA_QSCALE = LOG2E / math.sqrt(HEAD_DIM)
NA_QSCALE = 1.0 / math.sqrt(HEAD_DIM)
MASK_VALUE = -1e30

BF16 = jnp.bfloat16
F32 = jnp.float32
NT_DIMS = (((1,), (1,)), ((), ()))


def _dot(a, b):
    return jnp.dot(a, b, preferred_element_type=F32)


def _dot_nt(a, b):
    return lax.dot_general(a, b, NT_DIMS, preferred_element_type=F32)


def _rms(x, g):
    return x * lax.rsqrt(jnp.mean(x * x, axis=-1, keepdims=True) + EPS) * g


def _with_ones_rows(vt, n_heads):
    ones = jnp.ones((ONES_ROWS, vt.shape[1]), F32)
    parts = []
    for h in range(n_heads):
        parts += [vt[h * HEAD_DIM:(h + 1) * HEAD_DIM], ones]
    return jnp.concatenate(parts, axis=0)


def _compiler_params(n_axes):
    return pltpu.CompilerParams(
        dimension_semantics=("arbitrary",) * n_axes,
        vmem_limit_bytes=VMEM_LIMIT_BYTES,
    )


def _const_spec(shape):
    nd = len(shape)
    return pl.BlockSpec(shape, lambda *_: (0,) * nd, pipeline_mode=pl.Buffered(1))


def _pre_kernel(x_ref, gattn_ref, wna_ref, wmla_ref, wqc_ref, wkc_ref, wvct_ref,
                qan_ref, kvan_ref, wuq_ref, wuk_ref, wuvt_ref, gq_ref, gk_ref,
                cosm_ref, sinm_ref, cosa_ref, sina_ref,
                na_ref, mq_ref, mk_ref, mvt_ref, gqo_ref, gko_ref, gvt_ref):
    x = x_ref[0]
    xb = _rms(x, gattn_ref[...]).astype(BF16)

    na_ref[0] = _dot(xb, wna_ref[...]).astype(BF16)

    cosm = cosm_ref[...]
    sinm = sinm_ref[...]
    cosa = cosa_ref[...]
    sina = sina_ref[...]

    lat = _dot(xb, wmla_ref[...])
    cq = lat[:, :MLA_Q_RANK]
    ckv = lat[:, MLA_Q_RANK:MLA_Q_RANK + MLA_KV_RANK]
    o = MLA_Q_RANK + MLA_KV_RANK
    kpe = lat[:, o:o + LANES] * cosm + lat[:, o + LANES:o + 2 * LANES] * sinm

    cqn = _rms(cq, qan_ref[...]).astype(BF16)
    q2 = _dot(cqn, wuq_ref[...])
    hw = MLA_HEADS * LANES
    for h in range(MLA_HEADS):
        sl = slice(h * LANES, (h + 1) * LANES)
        qh = q2[:, sl] * cosm + q2[:, hw + h * LANES:hw + (h + 1) * LANES] * sinm
        mq_ref[0, :, sl] = (qh * MLA_QSCALE).astype(BF16)

    ckvn = _rms(ckv, kvan_ref[...]).astype(BF16)
    kn = _dot(ckvn, wuk_ref[...])
    for h in range(MLA_HEADS):
        sl = slice(h * LANES, (h + 1) * LANES)
        mk_ref[0, :, sl] = (kn[:, sl] + kpe).astype(BF16)
    mvt_ref[0, 0] = _with_ones_rows(_dot_nt(wuvt_ref[...], ckvn), MLA_HEADS).astype(BF16)

    gq = gq_ref[...]
    gqc = cosa * gq[0:1]
    gqs = sina * gq[1:2]
    qc2 = _dot(xb, wqc_ref[...])
    hw = GQA_HEADS * LANES
    for h in range(GQA_HEADS):
        sl = slice(h * LANES, (h + 1) * LANES)
        xh = qc2[:, sl]
        xr = qc2[:, hw + h * LANES:hw + (h + 1) * LANES]
        ms = jnp.sum(xh * xh, axis=-1, keepdims=True) * (1.0 / HEAD_DIM)
        sc = lax.rsqrt(ms + EPS) * GQA_QSCALE
        gqo_ref[0, :, sl] = ((xh * gqc + xr * gqs) * sc).astype(BF16)

    gk = gk_ref[...]
    kc2 = _dot(xb, wkc_ref[...])
    xk = kc2[:, :LANES]
    xkr = kc2[:, LANES:]
    lane = lax.broadcasted_iota(jnp.int32, xk.shape, 1)
    first = lane < HEAD_DIM
    sq = xk * xk
    ms0 = jnp.sum(jnp.where(first, sq, 0.0), axis=-1, keepdims=True) * (1.0 / HEAD_DIM)
    ms1 = jnp.sum(jnp.where(first, 0.0, sq), axis=-1, keepdims=True) * (1.0 / HEAD_DIM)
    sc = jnp.where(first, lax.rsqrt(ms0 + EPS), lax.rsqrt(ms1 + EPS))
    gko_ref[0] = ((xk * (cosa * gk[0:1]) + xkr * (sina * gk[1:2])) * sc).astype(BF16)
    gvt_ref[0, 0] = _with_ones_rows(_dot_nt(wvct_ref[...], xb), GQA_KV_HEADS).astype(BF16)


def _pre_call(x, lw, tabs):
    B, S, _ = x.shape
    tm = PRE_TM
    nt = S // tm
    tile = lambda w: pl.BlockSpec((1, tm, w), lambda b, i: (b, i, 0))
    tab = pl.BlockSpec((tm, LANES), lambda b, i: (i, 0))
    weights = [lw["g_attn"], lw["w_na"], lw["w_mla"], lw["w_qc"], lw["w_kc"], lw["w_vct"],
               lw["g_qa"], lw["g_kva"], lw["w_uq"], lw["w_uk"], lw["w_uvt"], lw["g_q"], lw["g_k"]]
    out_shape = (
        jax.ShapeDtypeStruct((B, S, 3 * NA_D), BF16),
        jax.ShapeDtypeStruct((B, S, MLA_HEADS * LANES), BF16),
        jax.ShapeDtypeStruct((B, S, MLA_HEADS * LANES), BF16),
        jax.ShapeDtypeStruct((B, nt, MLA_HEADS * V_ROWS, tm), BF16),
        jax.ShapeDtypeStruct((B, S, GQA_HEADS * LANES), BF16),
        jax.ShapeDtypeStruct((B, S, GQA_KV_HEADS * HEAD_DIM), BF16),
        jax.ShapeDtypeStruct((B, nt, GQA_KV_HEADS * V_ROWS, tm), BF16),
    )
    out_specs = (
        tile(3 * NA_D), tile(MLA_HEADS * LANES), tile(MLA_HEADS * LANES),
        pl.BlockSpec((1, 1, MLA_HEADS * V_ROWS, tm), lambda b, i: (b, i, 0, 0)),
        tile(GQA_HEADS * LANES), tile(GQA_KV_HEADS * HEAD_DIM),
        pl.BlockSpec((1, 1, GQA_KV_HEADS * V_ROWS, tm), lambda b, i: (b, i, 0, 0)),
    )
    return pl.pallas_call(
        _pre_kernel,
        grid=(B, nt),
        in_specs=[tile(D_MODEL)] + [_const_spec(w.shape) for w in weights] + [tab] * 4,
        out_specs=out_specs,
        out_shape=out_shape,
        compiler_params=_compiler_params(2),
        name="pre",
    )(x, *weights, tabs["cos_m"], tabs["sin_m"], tabs["cos_a"], tabs["sin_a"])


def _column_max(x):
    rows = x.shape[0]
    while rows > SUBLANES:
        rows //= 2
        x = jnp.maximum(x[:rows], x[rows:])
    return jnp.max(x, axis=0, keepdims=True)


def _attn_kernel(q_ref, k_ref, vt_ref, o_ref, m_ref, acc_ref, s_ref, cm_ref, *, n_heads,
                 k_tiles, v_rows, n_chunks, n_tiles, tq, tk):
    def reset():
        m_ref[...] = jnp.full(m_ref.shape, -jnp.inf, F32)
        acc_ref[...] = jnp.zeros(acc_ref.shape, F32)

    def scores(t, j, slot, h):
        row0 = pl.multiple_of(t * tq, tq)
        start = pl.multiple_of(j * tk, tk)
        kt = k_tiles[h]
        qh = q_ref[0, pl.ds(row0, tq), h * LANES:(h + 1) * LANES]
        kc = k_ref[0, pl.ds(start, tk), kt * LANES:(kt + 1) * LANES]
        st = _dot_nt(kc, qh)
        s_ref[slot, h] = st
        cm_ref[slot, h] = _column_max(st)

    def accumulate(j, slot, h):
        vr = v_rows[h]
        m_old = m_ref[h]
        m_new = jnp.maximum(m_old, cm_ref[slot, h])
        alpha = jnp.exp2(m_old - m_new)
        p = jnp.exp2(s_ref[slot, h] - m_new).astype(BF16)
        vc = vt_ref[0, j, vr * V_ROWS:(vr + 1) * V_ROWS, :]
        acc_ref[h] = alpha * acc_ref[h] + _dot(vc, p)
        m_ref[h] = m_new

    def step(t, j, slot, t_next, j_next):
        for h in range(n_heads):
            accumulate(j, slot, h)
            scores(t_next, j_next, 1 - slot, h)

    def tile_body(t, carry):
        def pair_body(jj, c):
            step(t, 2 * jj, 0, t, 2 * jj + 1)
            step(t, 2 * jj + 1, 1, t, 2 * jj + 2)
            return c

        lax.fori_loop(0, n_chunks // 2 - 1, pair_body, 0)
        step(t, n_chunks - 2, 0, t, n_chunks - 1)
        step(t, n_chunks - 1, 1, jnp.minimum(t + 1, n_tiles - 1), 0)
        outs = []
        for h in range(n_heads):
            acc = acc_ref[h]
            outs.append(acc[:HEAD_DIM] / acc[HEAD_DIM:HEAD_DIM + 1])
        o_ref[0, pl.ds(pl.multiple_of(t * tq, tq), tq), :] = jnp.concatenate(outs, axis=0).T
        reset()
        return carry

    reset()
    for h in range(n_heads):
        scores(0, 0, 0, h)
    lax.fori_loop(0, n_tiles, tile_body, 0)


def _attn_call(q, k, vt, *, k_tiles, v_rows, name):
    B, S, qw = q.shape
    n_heads = qw // LANES
    tq, tk, n_tiles = ATTN_TQ, ATTN_TK, ATTN_TILES
    n_chunks = S // tk
    rows = tq * n_tiles
    assert S % rows == 0 and S % tk == 0 and n_chunks % 2 == 0, (S, rows, tk)
    kw = k.shape[2]
    vw = vt.shape[2]
    ow = n_heads * HEAD_DIM
    kv_bytes = S * (kw + vw) * 2
    other_bytes = 2 * rows * (qw * 2 + ow * 4) + 2 * n_heads * tk * tq * 4
    kv_buffers = 2 if 2 * kv_bytes + other_bytes < ATTN_VMEM_BUDGET_BYTES else 1
    kern = functools.partial(_attn_kernel, n_heads=n_heads, k_tiles=k_tiles, v_rows=v_rows,
                             n_chunks=n_chunks, n_tiles=n_tiles, tq=tq, tk=tk)
    return pl.pallas_call(
        kern,
        grid=(B, S // rows),
        in_specs=[
            pl.BlockSpec((1, rows, qw), lambda b, i: (b, i, 0)),
            pl.BlockSpec((1, S, kw), lambda b, i: (b, 0, 0), pipeline_mode=pl.Buffered(kv_buffers)),
            pl.BlockSpec((1, n_chunks, vw, tk), lambda b, i: (b, 0, 0, 0),
                         pipeline_mode=pl.Buffered(kv_buffers)),
        ],
        out_specs=pl.BlockSpec((1, rows, ow), lambda b, i: (b, i, 0)),
        out_shape=jax.ShapeDtypeStruct((B, S, ow), F32),
        scratch_shapes=[pltpu.VMEM((n_heads, 1, tq), F32),
                        pltpu.VMEM((n_heads, V_ROWS, tq), F32),
                        pltpu.VMEM((2, n_heads, tk, tq), F32),
                        pltpu.VMEM((2, n_heads, 1, tq), F32)],
        compiler_params=_compiler_params(2),
        name=name,
    )(q, k, vt)


def _na_kernel(na_ref, bias_ref, o_ref, s_ref, *, rows, rb):
    i = pl.program_id(1)
    lane = lax.broadcasted_iota(jnp.int32, (GRID_W, LANES), 1)
    first = lane < HEAD_DIM
    n_pairs = NA_HEADS // 2

    def window_start(u):
        r = i * rb + u
        rs = jnp.clip(r - NA_ROWS // 2, 0, rows - NA_ROWS)
        return r, rs

    def scores(u, t):
        r, rs = window_start(u)
        kv0 = pl.multiple_of(rs * GRID_W, GRID_W)
        qt = na_ref[0, pl.ds(pl.multiple_of(r * GRID_W, GRID_W), GRID_W), t * LANES:(t + 1) * LANES]
        kt = na_ref[0, pl.ds(kv0, NA_KEYS), NA_D + t * LANES:NA_D + (t + 1) * LANES]
        zero = jnp.zeros_like(qt)
        qm = jnp.concatenate([jnp.where(first, qt, zero), jnp.where(first, zero, qt)], axis=0)
        s_ref[u % 2, t] = _dot_nt(qm, kt) + bias_ref[t, r - rs]

    def finish(u, t):
        _, rs = window_start(u)
        kv0 = pl.multiple_of(rs * GRID_W, GRID_W)
        vt = na_ref[0, pl.ds(kv0, NA_KEYS), 2 * NA_D + t * LANES:2 * NA_D + (t + 1) * LANES]
        s = s_ref[u % 2, t]
        m = jnp.max(s, axis=-1, keepdims=True)
        p = jnp.exp(s - m)
        l = jnp.sum(p, axis=-1, keepdims=True)
        o2 = _dot(p.astype(BF16), vt) / l
        o_ref[0, u * GRID_W:(u + 1) * GRID_W, t * LANES:(t + 1) * LANES] = (
            jnp.where(first, o2[:GRID_W], o2[GRID_W:]))

    for t in range(n_pairs):
        scores(0, t)
    for u in range(rb):
        for t in range(n_pairs):
            finish(u, t)
            if u + 1 < rb:
                scores(u + 1, t)


def _na_call(na, bias):
    B, S, w = na.shape
    rows = S // GRID_W
    rb = NA_RB
    kern = functools.partial(_na_kernel, rows=rows, rb=rb)
    return pl.pallas_call(
        kern,
        grid=(B, rows // rb),
        in_specs=[
            pl.BlockSpec((1, S, w), lambda b, i: (b, 0, 0)),
            _const_spec(bias.shape),
        ],
        out_specs=pl.BlockSpec((1, rb * GRID_W, NA_D), lambda b, i: (b, i, 0)),
        out_shape=jax.ShapeDtypeStruct((B, S, NA_D), F32),
        scratch_shapes=[pltpu.VMEM((2, NA_HEADS // 2, 2 * GRID_W, NA_KEYS), F32)],
        compiler_params=_compiler_params(2),
        name="na",
    )(na, bias)


def _post_kernel(x_ref, oa_ref, ob_ref, oc_ref, ga_ref, gb_ref, gc_ref, wout_ref, gffn_ref,
                 wg_ref, wu_ref, wd_ref, gfin_ref, o_ref, *, final):
    merged = jnp.concatenate(
        [_rms(oa_ref[...], ga_ref[...]), _rms(ob_ref[...], gb_ref[...]),
         _rms(oc_ref[...], gc_ref[...])], axis=-1).astype(BF16)
    x1 = x_ref[...] + _dot(merged, wout_ref[...])
    hb = _rms(x1, gffn_ref[...]).astype(BF16)
    g = _dot(hb, wg_ref[...])
    u = _dot(hb, wu_ref[...])
    act = (g * jax.nn.sigmoid(g) * u).astype(BF16)
    x2 = x1 + _dot(act, wd_ref[...])
    if final:
        x2 = _rms(x2, gfin_ref[...])
    o_ref[...] = x2


def _post_call(x, oa, ob, oc, lw, g_final, final):
    B, S, _ = x.shape
    T = B * S
    tm = POST_TM
    flat = lambda a: a.reshape(T, a.shape[-1])
    tile = lambda w: pl.BlockSpec((tm, w), lambda i: (i, 0))
    weights_a = [lw["g_a"], lw["g_b"], lw["g_c"], lw["w_out"], lw["g_ffn"],
                 lw["w_gate"], lw["w_up"], lw["w_down"], g_final]
    out = pl.pallas_call(
        functools.partial(_post_kernel, final=final),
        grid=(T // tm,),
        in_specs=[tile(D_MODEL), tile(NA_D), tile(MLA_HEADS * MLA_V), tile(GQA_HEADS * HEAD_DIM)]
        + [_const_spec(w.shape) for w in weights_a],
        out_specs=tile(D_MODEL),
        out_shape=jax.ShapeDtypeStruct((T, D_MODEL), F32),
        compiler_params=_compiler_params(1),
        name="post",
    )(flat(x), flat(oa), flat(ob), flat(oc), *weights_a)
    return out.reshape(B, S, D_MODEL)


def _rot32(w):
    return jnp.concatenate([-w[:, 16:32], w[:, 0:16]], axis=1)


def _rot64(w):
    return jnp.concatenate([_rot32(w[:, :32]), _rot32(w[:, 32:])], axis=1)


def _perm64(g):
    p = lambda v: jnp.concatenate([v[16:32], v[0:16]])
    return jnp.concatenate([p(g[:32]), p(g[32:])])


def _layer_weights(l, attn_norm, w_in, q_a_norm, kv_a_norm, w_uq, w_ukv, q_norm_c, k_norm_c,
                   g_out_a, g_out_b, g_out_c, w_out, ffn_norm, w_gate, w_up, w_down):
    wi = w_in[l]
    K = wi.shape[0]
    sizes = (NA_D, NA_D, NA_D, MLA_Q_RANK, MLA_KV_RANK, MLA_ROPE,
             GQA_HEADS * HEAD_DIM, GQA_KV_HEADS * HEAD_DIM, GQA_KV_HEADS * HEAD_DIM)
    offs = np.concatenate([[0], np.cumsum(sizes)])
    (wqa, wka, wva, wcq, wckv, wkpe, wqc, wkc, wvc) = [wi[:, offs[i]:offs[i + 1]] for i in range(9)]
    z = lambda n: jnp.zeros((K, n), F32)

    w_na = jnp.concatenate([wqa * NA_QSCALE, wka, wva], axis=1)
    kpe_tile = jnp.concatenate([z(MLA_NOPE), wkpe, z(LANES - MLA_QK)], axis=1)
    kpe_rot = jnp.concatenate([z(MLA_NOPE), _rot32(wkpe), z(LANES - MLA_QK)], axis=1)
    w_mla = jnp.concatenate([wcq, wckv, kpe_tile, kpe_rot], axis=1)

    qc_tiles, qc_rots = [], []
    for h in range(GQA_HEADS):
        wh = wqc[:, h * HEAD_DIM:(h + 1) * HEAD_DIM]
        pad = (lambda a: jnp.concatenate([a, z(HEAD_DIM)], axis=1)) if h // GQA_GROUP == 0 else \
              (lambda a: jnp.concatenate([z(HEAD_DIM), a], axis=1))
        qc_tiles.append(pad(wh))
        qc_rots.append(pad(_rot64(wh)))
    w_qc = jnp.concatenate(qc_tiles + qc_rots, axis=1)
    w_kc = jnp.concatenate(
        [wkc] + [_rot64(wkc[:, h * HEAD_DIM:(h + 1) * HEAD_DIM]) for h in range(GQA_KV_HEADS)], axis=1)

    uq = w_uq[l]
    zq = lambda n: jnp.zeros((MLA_Q_RANK, n), F32)
    uq_tiles, uq_rots = [], []
    for h in range(MLA_HEADS):
        wh = uq[:, h * MLA_QK:(h + 1) * MLA_QK]
        uq_tiles.append(jnp.concatenate([wh, zq(LANES - MLA_QK)], axis=1))
        uq_rots.append(jnp.concatenate([zq(MLA_NOPE), _rot32(wh[:, MLA_NOPE:]), zq(LANES - MLA_QK)], axis=1))
    w_uq_ext = jnp.concatenate(uq_tiles + uq_rots, axis=1)

    ukv = w_ukv[l]
    zk = jnp.zeros((MLA_KV_RANK, LANES - MLA_NOPE), F32)
    per = MLA_NOPE + MLA_V
    w_uk = jnp.concatenate(
        [jnp.concatenate([ukv[:, h * per:h * per + MLA_NOPE], zk], axis=1) for h in range(MLA_HEADS)], axis=1)
    w_uvt = jnp.concatenate([ukv[:, h * per + MLA_NOPE:(h + 1) * per] for h in range(MLA_HEADS)], axis=1).T

    two = lambda g: jnp.concatenate([g, g])
    row = lambda g: g.reshape(1, -1).astype(F32)
    return {
        "g_attn": row(attn_norm[l]),
        "w_na": w_na.astype(BF16), "w_mla": w_mla.astype(BF16), "w_qc": w_qc.astype(BF16),
        "w_kc": w_kc.astype(BF16), "w_vct": wvc.T.astype(BF16),
        "g_qa": row(q_a_norm[l]), "g_kva": row(kv_a_norm[l]),
        "w_uq": w_uq_ext.astype(BF16), "w_uk": w_uk.astype(BF16), "w_uvt": w_uvt.astype(BF16),
        "g_q": jnp.stack([two(q_norm_c[l]), two(_perm64(q_norm_c[l]))]).astype(F32),
        "g_k": jnp.stack([two(k_norm_c[l]), two(_perm64(k_norm_c[l]))]).astype(F32),
        "g_a": row(g_out_a[l]), "g_b": row(g_out_b[l]), "g_c": row(g_out_c[l]),
        "w_out": w_out[l].astype(BF16), "g_ffn": row(ffn_norm[l]),
        "w_gate": w_gate[l].astype(BF16), "w_up": w_up[l].astype(BF16), "w_down": w_down[l].astype(BF16),
    }


def _rope_tables(S):
    tok = jnp.arange(S)
    t = tok.astype(F32)
    row = (tok // GRID_W).astype(F32)
    col = (tok % GRID_W).astype(F32)
    half = MLA_ROPE // 2
    inv = ROPE_BASE ** (-(jnp.arange(half, dtype=F32) * 2.0) / MLA_ROPE)

    def cs(pos):
        ang = pos[:, None] * inv[None, :]
        c, s = jnp.cos(ang), jnp.sin(ang)
        return jnp.concatenate([c, c], axis=1), jnp.concatenate([s, s], axis=1)

    ct, st = cs(t)
    ones = jnp.ones((S, MLA_NOPE), F32)
    zeros = jnp.zeros((S, MLA_NOPE), F32)
    pad1 = jnp.ones((S, LANES - MLA_QK), F32)
    pad0 = jnp.zeros((S, LANES - MLA_QK), F32)
    cr, sr = cs(row)
    cc, sc = cs(col)
    ca = jnp.concatenate([cr, cc], axis=1)
    sa = jnp.concatenate([sr, sc], axis=1)
    return {
        "cos_m": jnp.concatenate([ones, ct, pad1], axis=1),
        "sin_m": jnp.concatenate([zeros, st, pad0], axis=1),
        "cos_a": jnp.concatenate([ca, ca], axis=1),
        "sin_a": jnp.concatenate([sa, sa], axis=1),
    }


def _na_bias_table(rpb_l):
    c = np.arange(GRID_W)
    cstart = np.clip(c - NA_COLS // 2, 0, GRID_W - NA_COLS)
    cp = np.arange(GRID_W)
    dc = cp[None, :] - c[:, None] + (NA_COLS - 1)
    valid = (cp[None, :] >= cstart[:, None]) & (cp[None, :] < cstart[:, None] + NA_COLS)
    onehot = ((dc[:, :, None] == np.arange(2 * NA_COLS - 1)) & valid[:, :, None]).astype(np.float32)
    t = jnp.einsum("hrd,cpd->hrcp", rpb_l.astype(F32), onehot, precision=lax.Precision.HIGHEST)
    t = jnp.where(valid[None, None], t, MASK_VALUE)
    b = jnp.stack([t[:, NA_ROWS - 1 - d:2 * NA_ROWS - 1 - d] for d in range(NA_ROWS)], axis=1)
    b = b.transpose(0, 1, 3, 2, 4).reshape(NA_HEADS // 2, 2, NA_ROWS, GRID_W, NA_KEYS)
    return b.transpose(0, 2, 1, 3, 4).reshape(NA_HEADS // 2, NA_ROWS, 2 * GRID_W, NA_KEYS)


MLA_K_TILES = tuple(range(MLA_HEADS))
MLA_V_ROWS = tuple(range(MLA_HEADS))
GQA_K_TILES = (0,) * GQA_HEADS
GQA_V_ROWS = tuple(h // GQA_GROUP for h in range(GQA_HEADS))


def _trunk(x, layers, biases, tabs, g_final):
    n_layers = len(layers)
    for l, lw in enumerate(layers):
        na, mq, mk, mvt, gq, gk, gvt = _pre_call(x, lw, tabs)
        oa = _na_call(na, biases[l])
        ob = _attn_call(mq, mk, mvt, k_tiles=MLA_K_TILES, v_rows=MLA_V_ROWS, name="mla")
        oc = _attn_call(gq, gk, gvt, k_tiles=GQA_K_TILES, v_rows=GQA_V_ROWS, name="gqa")
        x = _post_call(x, oa, ob, oc, lw, g_final, final=(l == n_layers - 1))
    return x


def kernel(x_prompt, x_sample, attn_norm, w_in, q_a_norm, kv_a_norm, w_uq, w_ukv, q_norm_c, k_norm_c, rpb, g_out_a, g_out_b, g_out_c, w_out, ffn_norm, w_gate, w_up, w_down, final_norm):
    depth = w_in.shape[0]
    layers = [_layer_weights(l, attn_norm, w_in, q_a_norm, kv_a_norm, w_uq, w_ukv, q_norm_c,
                             k_norm_c, g_out_a, g_out_b, g_out_c, w_out, ffn_norm, w_gate,
                             w_up, w_down) for l in range(depth)]
    biases = [_na_bias_table(rpb[l]) for l in range(depth)]
    g_final = final_norm.reshape(1, -1).astype(F32)
    outs = []
    for x in (x_prompt, x_sample):
        tabs = _rope_tables(x.shape[1])
        outs.append(_trunk(x, layers, biases, tabs, g_final))
    return tuple(outs)
```

```python
import functools
import math

import jax
import jax.numpy as jnp
import numpy as np
from jax import lax
from jax.experimental import pallas as pl
from jax.experimental.pallas import tpu as pltpu

D_MODEL = 1024
GRID_W = 64
HEAD_DIM = 64
NA_HEADS = 4
NA_ROWS = 8
NA_COLS = 16
NA_D = NA_HEADS * HEAD_DIM
MLA_HEADS = 6
MLA_Q_RANK = 384
MLA_KV_RANK = 256
MLA_NOPE = 64
MLA_ROPE = 32
MLA_V = 64
MLA_QK = MLA_NOPE + MLA_ROPE
GQA_HEADS = 6
GQA_KV_HEADS = 2
GQA_GROUP = GQA_HEADS // GQA_KV_HEADS
D_FF = 2816
ROPE_BASE = 10000.0
EPS = 1e-6

LANES = 128
SUBLANES = 8
VMEM_LIMIT_BYTES = 56 * 1024 * 1024
ATTN_VMEM_BUDGET_BYTES = 48 * 1024 * 1024

PRE_TM = 512
ATTN_TQ = 512
ATTN_TK = PRE_TM
ATTN_TILES = 4
ATTN_UNROLL = 4
ONES_ROWS = 16
V_ROWS = HEAD_DIM + ONES_ROWS
NA_RB = 8
NA_KEYS = NA_ROWS * GRID_W
POST_TM = 512

LOG2E = 1.4426950408889634
MLA_QSCALE = LOG2E / math.sqrt(MLA_QK)
GQA_QSCALE = LOG2E / math.sqrt(HEAD_DIM)
NA_QSCALE = 1.0 / math.sqrt(HEAD_DIM)
MASK_VALUE = -1e30

BF16 = jnp.bfloat16
F32 = jnp.float32
NT_DIMS = (((1,), (1,)), ((), ()))


def _dot(a, b):
    return jnp.dot(a, b, preferred_element_type=F32)


def _dot_nt(a, b):
    return lax.dot_general(a, b, NT_DIMS, preferred_element_type=F32)


def _rms(x, g):
    return x * lax.rsqrt(jnp.mean(x * x, axis=-1, keepdims=True) + EPS) * g


def _with_ones_rows(vt, n_heads):
    ones = jnp.ones((ONES_ROWS, vt.shape[1]), F32)
    parts = []
    for h in range(n_heads):
        parts += [vt[h * HEAD_DIM:(h + 1) * HEAD_DIM], ones]
    return jnp.concatenate(parts, axis=0)


def _compiler_params(n_axes):
    return pltpu.CompilerParams(
        dimension_semantics=("arbitrary",) * n_axes,
        vmem_limit_bytes=VMEM_LIMIT_BYTES,
    )


def _const_spec(shape):
    nd = len(shape)
    return pl.BlockSpec(shape, lambda *_: (0,) * nd, pipeline_mode=pl.Buffered(1))


def _pre_kernel(x_ref, gattn_ref, wna_ref, wmla_ref, wqc_ref, wkc_ref, wvct_ref,
                qan_ref, kvan_ref, wuq_ref, wuk_ref, wuvt_ref, gq_ref, gk_ref,
                cosm_ref, sinm_ref, cosa_ref, sina_ref,
                na_ref, mq_ref, mk_ref, mvt_ref, gqo_ref, gko_ref, gvt_ref):
    x = x_ref[0]
    xb = _rms(x, gattn_ref[...]).astype(BF16)

    na_ref[0] = _dot(xb, wna_ref[...]).astype(BF16)

    cosm = cosm_ref[...]
    sinm = sinm_ref[...]
    cosa = cosa_ref[...]
    sina = sina_ref[...]

    lat = _dot(xb, wmla_ref[...])
    cq = lat[:, :MLA_Q_RANK]
    ckv = lat[:, MLA_Q_RANK:MLA_Q_RANK + MLA_KV_RANK]
    o = MLA_Q_RANK + MLA_KV_RANK
    kpe = lat[:, o:o + LANES] * cosm + lat[:, o + LANES:o + 2 * LANES] * sinm

    cqn = _rms(cq, qan_ref[...]).astype(BF16)
    q2 = _dot(cqn, wuq_ref[...])
    hw = MLA_HEADS * LANES
    for h in range(MLA_HEADS):
        sl = slice(h * LANES, (h + 1) * LANES)
        qh = q2[:, sl] * cosm + q2[:, hw + h * LANES:hw + (h + 1) * LANES] * sinm
        mq_ref[0, :, sl] = (qh * MLA_QSCALE).astype(BF16)

    ckvn = _rms(ckv, kvan_ref[...]).astype(BF16)
    kn = _dot(ckvn, wuk_ref[...])
    for h in range(MLA_HEADS):
        sl = slice(h * LANES, (h + 1) * LANES)
        mk_ref[0, :, sl] = (kn[:, sl] + kpe).astype(BF16)
    mvt_ref[0, 0] = _with_ones_rows(_dot_nt(wuvt_ref[...], ckvn), MLA_HEADS).astype(BF16)

    gq = gq_ref[...]
    gqc = cosa * gq[0:1]
    gqs = sina * gq[1:2]
    qc2 = _dot(xb, wqc_ref[...])
    hw = GQA_HEADS * LANES
    for h in range(GQA_HEADS):
        sl = slice(h * LANES, (h + 1) * LANES)
        xh = qc2[:, sl]
        xr = qc2[:, hw + h * LANES:hw + (h + 1) * LANES]
        ms = jnp.sum(xh * xh, axis=-1, keepdims=True) * (1.0 / HEAD_DIM)
        sc = lax.rsqrt(ms + EPS) * GQA_QSCALE
        gqo_ref[0, :, sl] = ((xh * gqc + xr * gqs) * sc).astype(BF16)

    gk = gk_ref[...]
    kc2 = _dot(xb, wkc_ref[...])
    xk = kc2[:, :LANES]
    xkr = kc2[:, LANES:]
    lane = lax.broadcasted_iota(jnp.int32, xk.shape, 1)
    first = lane < HEAD_DIM
    sq = xk * xk
    ms0 = jnp.sum(jnp.where(first, sq, 0.0), axis=-1, keepdims=True) * (1.0 / HEAD_DIM)
    ms1 = jnp.sum(jnp.where(first, 0.0, sq), axis=-1, keepdims=True) * (1.0 / HEAD_DIM)
    sc = jnp.where(first, lax.rsqrt(ms0 + EPS), lax.rsqrt(ms1 + EPS))
    gko_ref[0] = ((xk * (cosa * gk[0:1]) + xkr * (sina * gk[1:2])) * sc).astype(BF16)
    gvt_ref[0, 0] = _with_ones_rows(_dot_nt(wvct_ref[...], xb), GQA_KV_HEADS).astype(BF16)


def _pre_call(x, lw, tabs):
    B, S, _ = x.shape
    tm = PRE_TM
    nt = S // tm
    tile = lambda w: pl.BlockSpec((1, tm, w), lambda b, i: (b, i, 0))
    tab = pl.BlockSpec((tm, LANES), lambda b, i: (i, 0))
    weights = [lw["g_attn"], lw["w_na"], lw["w_mla"], lw["w_qc"], lw["w_kc"], lw["w_vct"],
               lw["g_qa"], lw["g_kva"], lw["w_uq"], lw["w_uk"], lw["w_uvt"], lw["g_q"], lw["g_k"]]
    out_shape = (
        jax.ShapeDtypeStruct((B, S, 3 * NA_D), BF16),
        jax.ShapeDtypeStruct((B, S, MLA_HEADS * LANES), BF16),
        jax.ShapeDtypeStruct((B, S, MLA_HEADS * LANES), BF16),
        jax.ShapeDtypeStruct((B, nt, MLA_HEADS * V_ROWS, tm), BF16),
        jax.ShapeDtypeStruct((B, S, GQA_HEADS * LANES), BF16),
        jax.ShapeDtypeStruct((B, S, GQA_KV_HEADS * HEAD_DIM), BF16),
        jax.ShapeDtypeStruct((B, nt, GQA_KV_HEADS * V_ROWS, tm), BF16),
    )
    out_specs = (
        tile(3 * NA_D), tile(MLA_HEADS * LANES), tile(MLA_HEADS * LANES),
        pl.BlockSpec((1, 1, MLA_HEADS * V_ROWS, tm), lambda b, i: (b, i, 0, 0)),
        tile(GQA_HEADS * LANES), tile(GQA_KV_HEADS * HEAD_DIM),
        pl.BlockSpec((1, 1, GQA_KV_HEADS * V_ROWS, tm), lambda b, i: (b, i, 0, 0)),
    )
    return pl.pallas_call(
        _pre_kernel,
        grid=(B, nt),
        in_specs=[tile(D_MODEL)] + [_const_spec(w.shape) for w in weights] + [tab] * 4,
        out_specs=out_specs,
        out_shape=out_shape,
        compiler_params=_compiler_params(2),
        name="pre",
    )(x, *weights, tabs["cos_m"], tabs["sin_m"], tabs["cos_a"], tabs["sin_a"])


def _column_max(x):
    rows = x.shape[0]
    while rows > SUBLANES:
        rows //= 2
        x = jnp.maximum(x[:rows], x[rows:])
    return jnp.max(x, axis=0, keepdims=True)


def _attn_kernel(q_ref, k_ref, vt_ref, o_ref, m_ref, acc_ref, s_ref, cm_ref, *, n_heads,
                 k_tiles, v_rows, n_chunks, n_tiles, tq, tk):
    def reset():
        m_ref[...] = jnp.full(m_ref.shape, -jnp.inf, F32)
        acc_ref[...] = jnp.zeros(acc_ref.shape, F32)

    def scores(t, j, slot, h):
        row0 = pl.multiple_of(t * tq, tq)
        start = pl.multiple_of(j * tk, tk)
        kt = k_tiles[h]
        qh = q_ref[0, pl.ds(row0, tq), h * LANES:(h + 1) * LANES]
        kc = k_ref[0, pl.ds(start, tk), kt * LANES:(kt + 1) * LANES]
        st = _dot_nt(kc, qh)
        s_ref[slot, h] = st
        cm_ref[slot, h] = _column_max(st)

    def accumulate(j, slot, h):
        vr = v_rows[h]
        m_old = m_ref[h]
        m_new = jnp.maximum(m_old, cm_ref[slot, h])
        alpha = jnp.exp2(m_old - m_new)
        p = jnp.exp2(s_ref[slot, h] - m_new).astype(BF16)
        vc = vt_ref[0, j, vr * V_ROWS:(vr + 1) * V_ROWS, :]
        acc_ref[h] = alpha * acc_ref[h] + _dot(vc, p)
        m_ref[h] = m_new

    def step(t, j, slot, t_next, j_next):
        for h in range(n_heads):
            accumulate(j, slot, h)
            scores(t_next, j_next, 1 - slot, h)

    def tile_body(t, carry):
        def chunks_body(jb, c):
            j0 = jb * ATTN_UNROLL
            for u in range(ATTN_UNROLL - 1):
                step(t, j0 + u, u % 2, t, j0 + u + 1)
            j_last = j0 + ATTN_UNROLL - 1
            wrap = (j_last + 1 == n_chunks).astype(jnp.int32)
            step(t, j_last, (ATTN_UNROLL - 1) % 2,
                 jnp.minimum(t + wrap, n_tiles - 1), (1 - wrap) * (j_last + 1))
            return c

        lax.fori_loop(0, n_chunks // ATTN_UNROLL, chunks_body, 0)
        outs = []
        for h in range(n_heads):
            acc = acc_ref[h]
            outs.append(acc[:HEAD_DIM] / acc[HEAD_DIM:HEAD_DIM + 1])
        o_ref[0, pl.ds(pl.multiple_of(t * tq, tq), tq), :] = jnp.concatenate(outs, axis=0).T
        reset()
        return carry

    reset()
    for h in range(n_heads):
        scores(0, 0, 0, h)
    lax.fori_loop(0, n_tiles, tile_body, 0)


def _attn_call(q, k, vt, *, k_tiles, v_rows, name):
    B, S, qw = q.shape
    n_heads = qw // LANES
    tq, tk, n_tiles = ATTN_TQ, ATTN_TK, ATTN_TILES
    n_chunks = S // tk
    rows = tq * n_tiles
    assert S % rows == 0 and S % tk == 0 and n_chunks % ATTN_UNROLL == 0, (S, rows, tk)
    assert ATTN_UNROLL % 2 == 0
    kw = k.shape[2]
    vw = vt.shape[2]
    ow = n_heads * HEAD_DIM
    kv_bytes = S * (kw + vw) * 2
    other_bytes = 2 * rows * (qw * 2 + ow * 4) + 2 * n_heads * tk * tq * 4
    kv_buffers = 2 if 2 * kv_bytes + other_bytes < ATTN_VMEM_BUDGET_BYTES else 1
    kern = functools.partial(_attn_kernel, n_heads=n_heads, k_tiles=k_tiles, v_rows=v_rows,
                             n_chunks=n_chunks, n_tiles=n_tiles, tq=tq, tk=tk)
    return pl.pallas_call(
        kern,
        grid=(B, S // rows),
        in_specs=[
            pl.BlockSpec((1, rows, qw), lambda b, i: (b, i, 0)),
            pl.BlockSpec((1, S, kw), lambda b, i: (b, 0, 0), pipeline_mode=pl.Buffered(kv_buffers)),
            pl.BlockSpec((1, n_chunks, vw, tk), lambda b, i: (b, 0, 0, 0),
                         pipeline_mode=pl.Buffered(kv_buffers)),
        ],
        out_specs=pl.BlockSpec((1, rows, ow), lambda b, i: (b, i, 0)),
        out_shape=jax.ShapeDtypeStruct((B, S, ow), F32),
        scratch_shapes=[pltpu.VMEM((n_heads, 1, tq), F32),
                        pltpu.VMEM((n_heads, V_ROWS, tq), F32),
                        pltpu.VMEM((2, n_heads, tk, tq), F32),
                        pltpu.VMEM((2, n_heads, 1, tq), F32)],
        compiler_params=_compiler_params(2),
        name=name,
    )(q, k, vt)


def _na_kernel(na_ref, bias_ref, o_ref, s_ref, *, rows, rb):
    i = pl.program_id(1)
    lane = lax.broadcasted_iota(jnp.int32, (GRID_W, LANES), 1)
    first = lane < HEAD_DIM
    n_pairs = NA_HEADS // 2

    def window_start(u):
        r = i * rb + u
        rs = jnp.clip(r - NA_ROWS // 2, 0, rows - NA_ROWS)
        return r, rs

    def scores(u, t):
        r, rs = window_start(u)
        kv0 = pl.multiple_of(rs * GRID_W, GRID_W)
        qt = na_ref[0, pl.ds(pl.multiple_of(r * GRID_W, GRID_W), GRID_W), t * LANES:(t + 1) * LANES]
        kt = na_ref[0, pl.ds(kv0, NA_KEYS), NA_D + t * LANES:NA_D + (t + 1) * LANES]
        zero = jnp.zeros_like(qt)
        qm = jnp.concatenate([jnp.where(first, qt, zero), jnp.where(first, zero, qt)], axis=0)
        s_ref[u % 2, t] = _dot_nt(qm, kt) + bias_ref[t, r - rs]

    def finish(u, t):
        _, rs = window_start(u)
        kv0 = pl.multiple_of(rs * GRID_W, GRID_W)
        vt = na_ref[0, pl.ds(kv0, NA_KEYS), 2 * NA_D + t * LANES:2 * NA_D + (t + 1) * LANES]
        s = s_ref[u % 2, t]
        m = jnp.max(s, axis=-1, keepdims=True)
        p = jnp.exp(s - m)
        l = jnp.sum(p, axis=-1, keepdims=True)
        o2 = _dot(p.astype(BF16), vt) / l
        o_ref[0, u * GRID_W:(u + 1) * GRID_W, t * LANES:(t + 1) * LANES] = (
            jnp.where(first, o2[:GRID_W], o2[GRID_W:]))

    for t in range(n_pairs):
        scores(0, t)
    for u in range(rb):
        for t in range(n_pairs):
            finish(u, t)
            if u + 1 < rb:
                scores(u + 1, t)


def _na_call(na, bias):
    B, S, w = na.shape
    rows = S // GRID_W
    rb = NA_RB
    kern = functools.partial(_na_kernel, rows=rows, rb=rb)
    return pl.pallas_call(
        kern,
        grid=(B, rows // rb),
        in_specs=[
            pl.BlockSpec((1, S, w), lambda b, i: (b, 0, 0)),
            _const_spec(bias.shape),
        ],
        out_specs=pl.BlockSpec((1, rb * GRID_W, NA_D), lambda b, i: (b, i, 0)),
        out_shape=jax.ShapeDtypeStruct((B, S, NA_D), F32),
        scratch_shapes=[pltpu.VMEM((2, NA_HEADS // 2, 2 * GRID_W, NA_KEYS), F32)],
        compiler_params=_compiler_params(2),
        name="na",
    )(na, bias)


def _post_kernel(x_ref, oa_ref, ob_ref, oc_ref, ga_ref, gb_ref, gc_ref, wout_ref, gffn_ref,
                 wg_ref, wu_ref, wd_ref, gfin_ref, o_ref, *, final):
    merged = jnp.concatenate(
        [_rms(oa_ref[...], ga_ref[...]), _rms(ob_ref[...], gb_ref[...]),
         _rms(oc_ref[...], gc_ref[...])], axis=-1).astype(BF16)
    x1 = x_ref[...] + _dot(merged, wout_ref[...])
    hb = _rms(x1, gffn_ref[...]).astype(BF16)
    g = _dot(hb, wg_ref[...])
    u = _dot(hb, wu_ref[...])
    act = (g * jax.nn.sigmoid(g) * u).astype(BF16)
    x2 = x1 + _dot(act, wd_ref[...])
    if final:
        x2 = _rms(x2, gfin_ref[...])
    o_ref[...] = x2


def _post_call(x, oa, ob, oc, lw, g_final, final):
    B, S, _ = x.shape
    T = B * S
    tm = POST_TM
    flat = lambda a: a.reshape(T, a.shape[-1])
    tile = lambda w: pl.BlockSpec((tm, w), lambda i: (i, 0))
    weights_a = [lw["g_a"], lw["g_b"], lw["g_c"], lw["w_out"], lw["g_ffn"],
                 lw["w_gate"], lw["w_up"], lw["w_down"], g_final]
    out = pl.pallas_call(
        functools.partial(_post_kernel, final=final),
        grid=(T // tm,),
        in_specs=[tile(D_MODEL), tile(NA_D), tile(MLA_HEADS * MLA_V), tile(GQA_HEADS * HEAD_DIM)]
        + [_const_spec(w.shape) for w in weights_a],
        out_specs=tile(D_MODEL),
        out_shape=jax.ShapeDtypeStruct((T, D_MODEL), F32),
        compiler_params=_compiler_params(1),
        name="post",
    )(flat(x), flat(oa), flat(ob), flat(oc), *weights_a)
    return out.reshape(B, S, D_MODEL)


def _rot32(w):
    return jnp.concatenate([-w[:, 16:32], w[:, 0:16]], axis=1)


def _rot64(w):
    return jnp.concatenate([_rot32(w[:, :32]), _rot32(w[:, 32:])], axis=1)


def _perm64(g):
    p = lambda v: jnp.concatenate([v[16:32], v[0:16]])
    return jnp.concatenate([p(g[:32]), p(g[32:])])


def _layer_weights(l, attn_norm, w_in, q_a_norm, kv_a_norm, w_uq, w_ukv, q_norm_c, k_norm_c,
                   g_out_a, g_out_b, g_out_c, w_out, ffn_norm, w_gate, w_up, w_down):
    wi = w_in[l]
    K = wi.shape[0]
    sizes = (NA_D, NA_D, NA_D, MLA_Q_RANK, MLA_KV_RANK, MLA_ROPE,
             GQA_HEADS * HEAD_DIM, GQA_KV_HEADS * HEAD_DIM, GQA_KV_HEADS * HEAD_DIM)
    offs = np.concatenate([[0], np.cumsum(sizes)])
    (wqa, wka, wva, wcq, wckv, wkpe, wqc, wkc, wvc) = [wi[:, offs[i]:offs[i + 1]] for i in range(9)]
    z = lambda n: jnp.zeros((K, n), F32)

    w_na = jnp.concatenate([wqa * NA_QSCALE, wka, wva], axis=1)
    kpe_tile = jnp.concatenate([z(MLA_NOPE), wkpe, z(LANES - MLA_QK)], axis=1)
    kpe_rot = jnp.concatenate([z(MLA_NOPE), _rot32(wkpe), z(LANES - MLA_QK)], axis=1)
    w_mla = jnp.concatenate([wcq, wckv, kpe_tile, kpe_rot], axis=1)

    qc_tiles, qc_rots = [], []
    for h in range(GQA_HEADS):
        wh = wqc[:, h * HEAD_DIM:(h + 1) * HEAD_DIM]
        pad = (lambda a: jnp.concatenate([a, z(HEAD_DIM)], axis=1)) if h // GQA_GROUP == 0 else \
              (lambda a: jnp.concatenate([z(HEAD_DIM), a], axis=1))
        qc_tiles.append(pad(wh))
        qc_rots.append(pad(_rot64(wh)))
    w_qc = jnp.concatenate(qc_tiles + qc_rots, axis=1)
    w_kc = jnp.concatenate(
        [wkc] + [_rot64(wkc[:, h * HEAD_DIM:(h + 1) * HEAD_DIM]) for h in range(GQA_KV_HEADS)], axis=1)

    uq = w_uq[l]
    zq = lambda n: jnp.zeros((MLA_Q_RANK, n), F32)
    uq_tiles, uq_rots = [], []
    for h in range(MLA_HEADS):
        wh = uq[:, h * MLA_QK:(h + 1) * MLA_QK]
        uq_tiles.append(jnp.concatenate([wh, zq(LANES - MLA_QK)], axis=1))
        uq_rots.append(jnp.concatenate([zq(MLA_NOPE), _rot32(wh[:, MLA_NOPE:]), zq(LANES - MLA_QK)], axis=1))
    w_uq_ext = jnp.concatenate(uq_tiles + uq_rots, axis=1)

    ukv = w_ukv[l]
    zk = jnp.zeros((MLA_KV_RANK, LANES - MLA_NOPE), F32)
    per = MLA_NOPE + MLA_V
    w_uk = jnp.concatenate(
        [jnp.concatenate([ukv[:, h * per:h * per + MLA_NOPE], zk], axis=1) for h in range(MLA_HEADS)], axis=1)
    w_uvt = jnp.concatenate([ukv[:, h * per + MLA_NOPE:(h + 1) * per] for h in range(MLA_HEADS)], axis=1).T

    two = lambda g: jnp.concatenate([g, g])
    row = lambda g: g.reshape(1, -1).astype(F32)
    return {
        "g_attn": row(attn_norm[l]),
        "w_na": w_na.astype(BF16), "w_mla": w_mla.astype(BF16), "w_qc": w_qc.astype(BF16),
        "w_kc": w_kc.astype(BF16), "w_vct": wvc.T.astype(BF16),
        "g_qa": row(q_a_norm[l]), "g_kva": row(kv_a_norm[l]),
        "w_uq": w_uq_ext.astype(BF16), "w_uk": w_uk.astype(BF16), "w_uvt": w_uvt.astype(BF16),
        "g_q": jnp.stack([two(q_norm_c[l]), two(_perm64(q_norm_c[l]))]).astype(F32),
        "g_k": jnp.stack([two(k_norm_c[l]), two(_perm64(k_norm_c[l]))]).astype(F32),
        "g_a": row(g_out_a[l]), "g_b": row(g_out_b[l]), "g_c": row(g_out_c[l]),
        "w_out": w_out[l].astype(BF16), "g_ffn": row(ffn_norm[l]),
        "w_gate": w_gate[l].astype(BF16), "w_up": w_up[l].astype(BF16), "w_down": w_down[l].astype(BF16),
    }


def _rope_tables(S):
    tok = jnp.arange(S)
    t = tok.astype(F32)
    row = (tok // GRID_W).astype(F32)
    col = (tok % GRID_W).astype(F32)
    half = MLA_ROPE // 2
    inv = ROPE_BASE ** (-(jnp.arange(half, dtype=F32) * 2.0) / MLA_ROPE)

    def cs(pos):
        ang = pos[:, None] * inv[None, :]
        c, s = jnp.cos(ang), jnp.sin(ang)
        return jnp.concatenate([c, c], axis=1), jnp.concatenate([s, s], axis=1)

    ct, st = cs(t)
    ones = jnp.ones((S, MLA_NOPE), F32)
    zeros = jnp.zeros((S, MLA_NOPE), F32)
    pad1 = jnp.ones((S, LANES - MLA_QK), F32)
    pad0 = jnp.zeros((S, LANES - MLA_QK), F32)
    cr, sr = cs(row)
    cc, sc = cs(col)
    ca = jnp.concatenate([cr, cc], axis=1)
    sa = jnp.concatenate([sr, sc], axis=1)
    return {
        "cos_m": jnp.concatenate([ones, ct, pad1], axis=1),
        "sin_m": jnp.concatenate([zeros, st, pad0], axis=1),
        "cos_a": jnp.concatenate([ca, ca], axis=1),
        "sin_a": jnp.concatenate([sa, sa], axis=1),
    }


def _na_bias_table(rpb_l):
    c = np.arange(GRID_W)
    cstart = np.clip(c - NA_COLS // 2, 0, GRID_W - NA_COLS)
    cp = np.arange(GRID_W)
    dc = cp[None, :] - c[:, None] + (NA_COLS - 1)
    valid = (cp[None, :] >= cstart[:, None]) & (cp[None, :] < cstart[:, None] + NA_COLS)
    onehot = ((dc[:, :, None] == np.arange(2 * NA_COLS - 1)) & valid[:, :, None]).astype(np.float32)
    t = jnp.einsum("hrd,cpd->hrcp", rpb_l.astype(F32), onehot, precision=lax.Precision.HIGHEST)
    t = jnp.where(valid[None, None], t, MASK_VALUE)
    b = jnp.stack([t[:, NA_ROWS - 1 - d:2 * NA_ROWS - 1 - d] for d in range(NA_ROWS)], axis=1)
    b = b.transpose(0, 1, 3, 2, 4).reshape(NA_HEADS // 2, 2, NA_ROWS, GRID_W, NA_KEYS)
    return b.transpose(0, 2, 1, 3, 4).reshape(NA_HEADS // 2, NA_ROWS, 2 * GRID_W, NA_KEYS)


MLA_K_TILES = tuple(range(MLA_HEADS))
MLA_V_ROWS = tuple(range(MLA_HEADS))
GQA_K_TILES = (0,) * GQA_HEADS
GQA_V_ROWS = tuple(h // GQA_GROUP for h in range(GQA_HEADS))


def _trunk(x, layers, biases, tabs, g_final):
    n_layers = len(layers)
    for l, lw in enumerate(layers):
        na, mq, mk, mvt, gq, gk, gvt = _pre_call(x, lw, tabs)
        oa = _na_call(na, biases[l])
        ob = _attn_call(mq, mk, mvt, k_tiles=MLA_K_TILES, v_rows=MLA_V_ROWS, name="mla")
        oc = _attn_call(gq, gk, gvt, k_tiles=GQA_K_TILES, v_rows=GQA_V_ROWS, name="gqa")
        x = _post_call(x, oa, ob, oc, lw, g_final, final=(l == n_layers - 1))
    return x


def kernel(x_prompt, x_sample, attn_norm, w_in, q_a_norm, kv_a_norm, w_uq, w_ukv, q_norm_c, k_norm_c, rpb, g_out_a, g_out_b, g_out_c, w_out, ffn_norm, w_gate, w_up, w_down, final_norm):
    depth = w_in.shape[0]
    layers = [_layer_weights(l, attn_norm, w_in, q_a_norm, kv_a_norm, w_uq, w_ukv, q_norm_c,
                             k_norm_c, g_out_a, g_out_b, g_out_c, w_out, ffn_norm, w_gate,
                             w_up, w_down) for l in range(depth)]
    biases = [_na_bias_table(rpb[l]) for l in range(depth)]
    g_final = final_norm.reshape(1, -1).astype(F32)
    outs = []
    for x in (x_prompt, x_sample):
        tabs = _rope_tables(x.shape[1])
        outs.append(_trunk(x, layers, biases, tabs, g_final))
    return tuple(outs)
```

```python
import functools
import math

import jax
import jax.numpy as jnp
import numpy as np
from jax import lax
from jax.experimental import pallas as pl
from jax.experimental.pallas import tpu as pltpu

D_MODEL = 1024
GRID_W = 64
HEAD_DIM = 64
NA_HEADS = 4
NA_ROWS = 8
NA_COLS = 16
NA_D = NA_HEADS * HEAD_DIM
MLA_HEADS = 6
MLA_Q_RANK = 384
MLA_KV_RANK = 256
MLA_NOPE = 64
MLA_ROPE = 32
MLA_V = 64
MLA_QK = MLA_NOPE + MLA_ROPE
GQA_HEADS = 6
GQA_KV_HEADS = 2
GQA_GROUP = GQA_HEADS // GQA_KV_HEADS
D_FF = 2816
ROPE_BASE = 10000.0
EPS = 1e-6

LANES = 128
SUBLANES = 8
VMEM_LIMIT_BYTES = 56 * 1024 * 1024
ATTN_VMEM_BUDGET_BYTES = 48 * 1024 * 1024

PRE_TM = 512
ATTN_TQ = 512
ATTN_TK = PRE_TM
ATTN_TILES = 4
ATTN_UNROLL = 4
BF16_EXP_ROWS = 256
ONES_ROWS = 16
V_ROWS = HEAD_DIM + ONES_ROWS
NA_RB = 8
NA_KEYS = NA_ROWS * GRID_W
POST_TM = 512

LOG2E = 1.4426950408889634
MLA_QSCALE = LOG2E / math.sqrt(MLA_QK)
GQA_QSCALE = LOG2E / math.sqrt(HEAD_DIM)
NA_QSCALE = 1.0 / math.sqrt(HEAD_DIM)
MASK_VALUE = -1e30

BF16 = jnp.bfloat16
F32 = jnp.float32
NT_DIMS = (((1,), (1,)), ((), ()))


def _dot(a, b):
    return jnp.dot(a, b, preferred_element_type=F32)


def _dot_nt(a, b):
    return lax.dot_general(a, b, NT_DIMS, preferred_element_type=F32)


def _rms(x, g):
    return x * lax.rsqrt(jnp.mean(x * x, axis=-1, keepdims=True) + EPS) * g


def _with_ones_rows(vt, n_heads):
    ones = jnp.ones((ONES_ROWS, vt.shape[1]), F32)
    parts = []
    for h in range(n_heads):
        parts += [vt[h * HEAD_DIM:(h + 1) * HEAD_DIM], ones]
    return jnp.concatenate(parts, axis=0)


def _compiler_params(n_axes):
    return pltpu.CompilerParams(
        dimension_semantics=("arbitrary",) * n_axes,
        vmem_limit_bytes=VMEM_LIMIT_BYTES,
    )


def _const_spec(shape):
    nd = len(shape)
    return pl.BlockSpec(shape, lambda *_: (0,) * nd, pipeline_mode=pl.Buffered(1))


def _pre_kernel(x_ref, gattn_ref, wna_ref, wmla_ref, wqc_ref, wkc_ref, wvct_ref,
                qan_ref, kvan_ref, wuq_ref, wuk_ref, wuvt_ref, gq_ref, gk_ref,
                cosm_ref, sinm_ref, cosa_ref, sina_ref,
                na_ref, mq_ref, mk_ref, mvt_ref, gqo_ref, gko_ref, gvt_ref):
    x = x_ref[0]
    xb = _rms(x, gattn_ref[...]).astype(BF16)

    na_ref[0] = _dot(xb, wna_ref[...]).astype(BF16)

    cosm = cosm_ref[...]
    sinm = sinm_ref[...]
    cosa = cosa_ref[...]
    sina = sina_ref[...]

    lat = _dot(xb, wmla_ref[...])
    cq = lat[:, :MLA_Q_RANK]
    ckv = lat[:, MLA_Q_RANK:MLA_Q_RANK + MLA_KV_RANK]
    o = MLA_Q_RANK + MLA_KV_RANK
    kpe = lat[:, o:o + LANES] * cosm + lat[:, o + LANES:o + 2 * LANES] * sinm

    cqn = _rms(cq, qan_ref[...]).astype(BF16)
    q2 = _dot(cqn, wuq_ref[...])
    hw = MLA_HEADS * LANES
    for h in range(MLA_HEADS):
        sl = slice(h * LANES, (h + 1) * LANES)
        qh = q2[:, sl] * cosm + q2[:, hw + h * LANES:hw + (h + 1) * LANES] * sinm
        mq_ref[0, :, sl] = (qh * MLA_QSCALE).astype(BF16)

    ckvn = _rms(ckv, kvan_ref[...]).astype(BF16)
    kn = _dot(ckvn, wuk_ref[...])
    for h in range(MLA_HEADS):
        sl = slice(h * LANES, (h + 1) * LANES)
        mk_ref[0, :, sl] = (kn[:, sl] + kpe).astype(BF16)
    mvt_ref[0, 0] = _with_ones_rows(_dot_nt(wuvt_ref[...], ckvn), MLA_HEADS).astype(BF16)

    gq = gq_ref[...]
    gqc = cosa * gq[0:1]
    gqs = sina * gq[1:2]
    qc2 = _dot(xb, wqc_ref[...])
    hw = GQA_HEADS * LANES
    for h in range(GQA_HEADS):
        sl = slice(h * LANES, (h + 1) * LANES)
        xh = qc2[:, sl]
        xr = qc2[:, hw + h * LANES:hw + (h + 1) * LANES]
        ms = jnp.sum(xh * xh, axis=-1, keepdims=True) * (1.0 / HEAD_DIM)
        sc = lax.rsqrt(ms + EPS) * GQA_QSCALE
        gqo_ref[0, :, sl] = ((xh * gqc + xr * gqs) * sc).astype(BF16)

    gk = gk_ref[...]
    kc2 = _dot(xb, wkc_ref[...])
    xk = kc2[:, :LANES]
    xkr = kc2[:, LANES:]
    lane = lax.broadcasted_iota(jnp.int32, xk.shape, 1)
    first = lane < HEAD_DIM
    sq = xk * xk
    ms0 = jnp.sum(jnp.where(first, sq, 0.0), axis=-1, keepdims=True) * (1.0 / HEAD_DIM)
    ms1 = jnp.sum(jnp.where(first, 0.0, sq), axis=-1, keepdims=True) * (1.0 / HEAD_DIM)
    sc = jnp.where(first, lax.rsqrt(ms0 + EPS), lax.rsqrt(ms1 + EPS))
    gko_ref[0] = ((xk * (cosa * gk[0:1]) + xkr * (sina * gk[1:2])) * sc).astype(BF16)
    gvt_ref[0, 0] = _with_ones_rows(_dot_nt(wvct_ref[...], xb), GQA_KV_HEADS).astype(BF16)


def _pre_call(x, lw, tabs):
    B, S, _ = x.shape
    tm = PRE_TM
    nt = S // tm
    tile = lambda w: pl.BlockSpec((1, tm, w), lambda b, i: (b, i, 0))
    tab = pl.BlockSpec((tm, LANES), lambda b, i: (i, 0))
    weights = [lw["g_attn"], lw["w_na"], lw["w_mla"], lw["w_qc"], lw["w_kc"], lw["w_vct"],
               lw["g_qa"], lw["g_kva"], lw["w_uq"], lw["w_uk"], lw["w_uvt"], lw["g_q"], lw["g_k"]]
    out_shape = (
        jax.ShapeDtypeStruct((B, S, 3 * NA_D), BF16),
        jax.ShapeDtypeStruct((B, S, MLA_HEADS * LANES), BF16),
        jax.ShapeDtypeStruct((B, S, MLA_HEADS * LANES), BF16),
        jax.ShapeDtypeStruct((B, nt, MLA_HEADS * V_ROWS, tm), BF16),
        jax.ShapeDtypeStruct((B, S, GQA_HEADS * LANES), BF16),
        jax.ShapeDtypeStruct((B, S, GQA_KV_HEADS * HEAD_DIM), BF16),
        jax.ShapeDtypeStruct((B, nt, GQA_KV_HEADS * V_ROWS, tm), BF16),
    )
    out_specs = (
        tile(3 * NA_D), tile(MLA_HEADS * LANES), tile(MLA_HEADS * LANES),
        pl.BlockSpec((1, 1, MLA_HEADS * V_ROWS, tm), lambda b, i: (b, i, 0, 0)),
        tile(GQA_HEADS * LANES), tile(GQA_KV_HEADS * HEAD_DIM),
        pl.BlockSpec((1, 1, GQA_KV_HEADS * V_ROWS, tm), lambda b, i: (b, i, 0, 0)),
    )
    return pl.pallas_call(
        _pre_kernel,
        grid=(B, nt),
        in_specs=[tile(D_MODEL)] + [_const_spec(w.shape) for w in weights] + [tab] * 4,
        out_specs=out_specs,
        out_shape=out_shape,
        compiler_params=_compiler_params(2),
        name="pre",
    )(x, *weights, tabs["cos_m"], tabs["sin_m"], tabs["cos_a"], tabs["sin_a"])


def _column_max(x):
    rows = x.shape[0]
    while rows > SUBLANES:
        rows //= 2
        x = jnp.maximum(x[:rows], x[rows:])
    return jnp.max(x, axis=0, keepdims=True)


def _attn_kernel(q_ref, k_ref, vt_ref, o_ref, m_ref, acc_ref, s_ref, cm_ref, *, n_heads,
                 k_tiles, v_rows, n_chunks, n_tiles, tq, tk):
    def reset():
        m_ref[...] = jnp.full(m_ref.shape, -jnp.inf, F32)
        acc_ref[...] = jnp.zeros(acc_ref.shape, F32)

    def scores(t, j, slot, h):
        row0 = pl.multiple_of(t * tq, tq)
        start = pl.multiple_of(j * tk, tk)
        kt = k_tiles[h]
        qh = q_ref[0, pl.ds(row0, tq), h * LANES:(h + 1) * LANES]
        kc = k_ref[0, pl.ds(start, tk), kt * LANES:(kt + 1) * LANES]
        st = _dot_nt(kc, qh)
        s_ref[slot, h] = st
        cm_ref[slot, h] = _column_max(st)

    def accumulate(j, slot, h):
        vr = v_rows[h]
        m_old = m_ref[h]
        m_new = jnp.maximum(m_old, cm_ref[slot, h])
        alpha = jnp.exp2(m_old - m_new)
        d = s_ref[slot, h] - m_new
        split = tk - BF16_EXP_ROWS
        p = jnp.concatenate([jnp.exp2(d[:split]).astype(BF16), jnp.exp2(d[split:].astype(BF16))],
                            axis=0)
        vc = vt_ref[0, j, vr * V_ROWS:(vr + 1) * V_ROWS, :]
        acc_ref[h] = alpha * acc_ref[h] + _dot(vc, p)
        m_ref[h] = m_new

    def step(t, j, slot, t_next, j_next):
        for h in range(n_heads):
            accumulate(j, slot, h)
            scores(t_next, j_next, 1 - slot, h)

    def tile_body(t, carry):
        def chunks_body(jb, c):
            j0 = jb * ATTN_UNROLL
            for u in range(ATTN_UNROLL - 1):
                step(t, j0 + u, u % 2, t, j0 + u + 1)
            j_last = j0 + ATTN_UNROLL - 1
            wrap = (j_last + 1 == n_chunks).astype(jnp.int32)
            step(t, j_last, (ATTN_UNROLL - 1) % 2,
                 jnp.minimum(t + wrap, n_tiles - 1), (1 - wrap) * (j_last + 1))
            return c

        lax.fori_loop(0, n_chunks // ATTN_UNROLL, chunks_body, 0)
        outs = []
        for h in range(n_heads):
            acc = acc_ref[h]
            outs.append(acc[:HEAD_DIM] / acc[HEAD_DIM:HEAD_DIM + 1])
        o_ref[0, pl.ds(pl.multiple_of(t * tq, tq), tq), :] = jnp.concatenate(outs, axis=0).T
        reset()
        return carry

    reset()
    for h in range(n_heads):
        scores(0, 0, 0, h)
    lax.fori_loop(0, n_tiles, tile_body, 0)


def _attn_call(q, k, vt, *, k_tiles, v_rows, name):
    B, S, qw = q.shape
    n_heads = qw // LANES
    tq, tk, n_tiles = ATTN_TQ, ATTN_TK, ATTN_TILES
    n_chunks = S // tk
    rows = tq * n_tiles
    assert S % rows == 0 and S % tk == 0 and n_chunks % ATTN_UNROLL == 0, (S, rows, tk)
    assert ATTN_UNROLL % 2 == 0
    kw = k.shape[2]
    vw = vt.shape[2]
    ow = n_heads * HEAD_DIM
    kv_bytes = S * (kw + vw) * 2
    other_bytes = 2 * rows * (qw * 2 + ow * 4) + 2 * n_heads * tk * tq * 4
    kv_buffers = 2 if 2 * kv_bytes + other_bytes < ATTN_VMEM_BUDGET_BYTES else 1
    kern = functools.partial(_attn_kernel, n_heads=n_heads, k_tiles=k_tiles, v_rows=v_rows,
                             n_chunks=n_chunks, n_tiles=n_tiles, tq=tq, tk=tk)
    return pl.pallas_call(
        kern,
        grid=(B, S // rows),
        in_specs=[
            pl.BlockSpec((1, rows, qw), lambda b, i: (b, i, 0)),
            pl.BlockSpec((1, S, kw), lambda b, i: (b, 0, 0), pipeline_mode=pl.Buffered(kv_buffers)),
            pl.BlockSpec((1, n_chunks, vw, tk), lambda b, i: (b, 0, 0, 0),
                         pipeline_mode=pl.Buffered(kv_buffers)),
        ],
        out_specs=pl.BlockSpec((1, rows, ow), lambda b, i: (b, i, 0)),
        out_shape=jax.ShapeDtypeStruct((B, S, ow), F32),
        scratch_shapes=[pltpu.VMEM((n_heads, 1, tq), F32),
                        pltpu.VMEM((n_heads, V_ROWS, tq), F32),
                        pltpu.VMEM((2, n_heads, tk, tq), F32),
                        pltpu.VMEM((2, n_heads, 1, tq), F32)],
        compiler_params=_compiler_params(2),
        name=name,
    )(q, k, vt)


def _na_kernel(na_ref, bias_ref, o_ref, s_ref, *, rows, rb):
    i = pl.program_id(1)
    lane = lax.broadcasted_iota(jnp.int32, (GRID_W, LANES), 1)
    first = lane < HEAD_DIM
    n_pairs = NA_HEADS // 2

    def window_start(u):
        r = i * rb + u
        rs = jnp.clip(r - NA_ROWS // 2, 0, rows - NA_ROWS)
        return r, rs

    def scores(u, t):
        r, rs = window_start(u)
        kv0 = pl.multiple_of(rs * GRID_W, GRID_W)
        qt = na_ref[0, pl.ds(pl.multiple_of(r * GRID_W, GRID_W), GRID_W), t * LANES:(t + 1) * LANES]
        kt = na_ref[0, pl.ds(kv0, NA_KEYS), NA_D + t * LANES:NA_D + (t + 1) * LANES]
        zero = jnp.zeros_like(qt)
        qm = jnp.concatenate([jnp.where(first, qt, zero), jnp.where(first, zero, qt)], axis=0)
        s_ref[u % 2, t] = _dot_nt(qm, kt) + bias_ref[t, r - rs]

    def finish(u, t):
        _, rs = window_start(u)
        kv0 = pl.multiple_of(rs * GRID_W, GRID_W)
        vt = na_ref[0, pl.ds(kv0, NA_KEYS), 2 * NA_D + t * LANES:2 * NA_D + (t + 1) * LANES]
        s = s_ref[u % 2, t]
        m = jnp.max(s, axis=-1, keepdims=True)
        p = jnp.exp(s - m)
        l = jnp.sum(p, axis=-1, keepdims=True)
        o2 = _dot(p.astype(BF16), vt) / l
        o_ref[0, u * GRID_W:(u + 1) * GRID_W, t * LANES:(t + 1) * LANES] = (
            jnp.where(first, o2[:GRID_W], o2[GRID_W:]))

    for t in range(n_pairs):
        scores(0, t)
    for u in range(rb):
        for t in range(n_pairs):
            finish(u, t)
            if u + 1 < rb:
                scores(u + 1, t)


def _na_call(na, bias):
    B, S, w = na.shape
    rows = S // GRID_W
    rb = NA_RB
    kern = functools.partial(_na_kernel, rows=rows, rb=rb)
    return pl.pallas_call(
        kern,
        grid=(B, rows // rb),
        in_specs=[
            pl.BlockSpec((1, S, w), lambda b, i: (b, 0, 0)),
            _const_spec(bias.shape),
        ],
        out_specs=pl.BlockSpec((1, rb * GRID_W, NA_D), lambda b, i: (b, i, 0)),
        out_shape=jax.ShapeDtypeStruct((B, S, NA_D), F32),
        scratch_shapes=[pltpu.VMEM((2, NA_HEADS // 2, 2 * GRID_W, NA_KEYS), F32)],
        compiler_params=_compiler_params(2),
        name="na",
    )(na, bias)


def _post_kernel(x_ref, oa_ref, ob_ref, oc_ref, ga_ref, gb_ref, gc_ref, wout_ref, gffn_ref,
                 wg_ref, wu_ref, wd_ref, gfin_ref, o_ref, *, final):
    merged = jnp.concatenate(
        [_rms(oa_ref[...], ga_ref[...]), _rms(ob_ref[...], gb_ref[...]),
         _rms(oc_ref[...], gc_ref[...])], axis=-1).astype(BF16)
    x1 = x_ref[...] + _dot(merged, wout_ref[...])
    hb = _rms(x1, gffn_ref[...]).astype(BF16)
    g = _dot(hb, wg_ref[...])
    u = _dot(hb, wu_ref[...])
    act = (g * jax.nn.sigmoid(g) * u).astype(BF16)
    x2 = x1 + _dot(act, wd_ref[...])
    if final:
        x2 = _rms(x2, gfin_ref[...])
    o_ref[...] = x2


def _post_call(x, oa, ob, oc, lw, g_final, final):
    B, S, _ = x.shape
    T = B * S
    tm = POST_TM
    flat = lambda a: a.reshape(T, a.shape[-1])
    tile = lambda w: pl.BlockSpec((tm, w), lambda i: (i, 0))
    weights_a = [lw["g_a"], lw["g_b"], lw["g_c"], lw["w_out"], lw["g_ffn"],
                 lw["w_gate"], lw["w_up"], lw["w_down"], g_final]
    out = pl.pallas_call(
        functools.partial(_post_kernel, final=final),
        grid=(T // tm,),
        in_specs=[tile(D_MODEL), tile(NA_D), tile(MLA_HEADS * MLA_V), tile(GQA_HEADS * HEAD_DIM)]
        + [_const_spec(w.shape) for w in weights_a],
        out_specs=tile(D_MODEL),
        out_shape=jax.ShapeDtypeStruct((T, D_MODEL), F32),
        compiler_params=_compiler_params(1),
        name="post",
    )(flat(x), flat(oa), flat(ob), flat(oc), *weights_a)
    return out.reshape(B, S, D_MODEL)


def _rot32(w):
    return jnp.concatenate([-w[:, 16:32], w[:, 0:16]], axis=1)


def _rot64(w):
    return jnp.concatenate([_rot32(w[:, :32]), _rot32(w[:, 32:])], axis=1)


def _perm64(g):
    p = lambda v: jnp.concatenate([v[16:32], v[0:16]])
    return jnp.concatenate([p(g[:32]), p(g[32:])])


def _layer_weights(l, attn_norm, w_in, q_a_norm, kv_a_norm, w_uq, w_ukv, q_norm_c, k_norm_c,
                   g_out_a, g_out_b, g_out_c, w_out, ffn_norm, w_gate, w_up, w_down):
    wi = w_in[l]
    K = wi.shape[0]
    sizes = (NA_D, NA_D, NA_D, MLA_Q_RANK, MLA_KV_RANK, MLA_ROPE,
             GQA_HEADS * HEAD_DIM, GQA_KV_HEADS * HEAD_DIM, GQA_KV_HEADS * HEAD_DIM)
    offs = np.concatenate([[0], np.cumsum(sizes)])
    (wqa, wka, wva, wcq, wckv, wkpe, wqc, wkc, wvc) = [wi[:, offs[i]:offs[i + 1]] for i in range(9)]
    z = lambda n: jnp.zeros((K, n), F32)

    w_na = jnp.concatenate([wqa * NA_QSCALE, wka, wva], axis=1)
    kpe_tile = jnp.concatenate([z(MLA_NOPE), wkpe, z(LANES - MLA_QK)], axis=1)
    kpe_rot = jnp.concatenate([z(MLA_NOPE), _rot32(wkpe), z(LANES - MLA_QK)], axis=1)
    w_mla = jnp.concatenate([wcq, wckv, kpe_tile, kpe_rot], axis=1)

    qc_tiles, qc_rots = [], []
    for h in range(GQA_HEADS):
        wh = wqc[:, h * HEAD_DIM:(h + 1) * HEAD_DIM]
        pad = (lambda a: jnp.concatenate([a, z(HEAD_DIM)], axis=1)) if h // GQA_GROUP == 0 else \
              (lambda a: jnp.concatenate([z(HEAD_DIM), a], axis=1))
        qc_tiles.append(pad(wh))
        qc_rots.append(pad(_rot64(wh)))
    w_qc = jnp.concatenate(qc_tiles + qc_rots, axis=1)
    w_kc = jnp.concatenate(
        [wkc] + [_rot64(wkc[:, h * HEAD_DIM:(h + 1) * HEAD_DIM]) for h in range(GQA_KV_HEADS)], axis=1)

    uq = w_uq[l]
    zq = lambda n: jnp.zeros((MLA_Q_RANK, n), F32)
    uq_tiles, uq_rots = [], []
    for h in range(MLA_HEADS):
        wh = uq[:, h * MLA_QK:(h + 1) * MLA_QK]
        uq_tiles.append(jnp.concatenate([wh, zq(LANES - MLA_QK)], axis=1))
        uq_rots.append(jnp.concatenate([zq(MLA_NOPE), _rot32(wh[:, MLA_NOPE:]), zq(LANES - MLA_QK)], axis=1))
    w_uq_ext = jnp.concatenate(uq_tiles + uq_rots, axis=1)

    ukv = w_ukv[l]
    zk = jnp.zeros((MLA_KV_RANK, LANES - MLA_NOPE), F32)
    per = MLA_NOPE + MLA_V
    w_uk = jnp.concatenate(
        [jnp.concatenate([ukv[:, h * per:h * per + MLA_NOPE], zk], axis=1) for h in range(MLA_HEADS)], axis=1)
    w_uvt = jnp.concatenate([ukv[:, h * per + MLA_NOPE:(h + 1) * per] for h in range(MLA_HEADS)], axis=1).T

    two = lambda g: jnp.concatenate([g, g])
    row = lambda g: g.reshape(1, -1).astype(F32)
    return {
        "g_attn": row(attn_norm[l]),
        "w_na": w_na.astype(BF16), "w_mla": w_mla.astype(BF16), "w_qc": w_qc.astype(BF16),
        "w_kc": w_kc.astype(BF16), "w_vct": wvc.T.astype(BF16),
        "g_qa": row(q_a_norm[l]), "g_kva": row(kv_a_norm[l]),
        "w_uq": w_uq_ext.astype(BF16), "w_uk": w_uk.astype(BF16), "w_uvt": w_uvt.astype(BF16),
        "g_q": jnp.stack([two(q_norm_c[l]), two(_perm64(q_norm_c[l]))]).astype(F32),
        "g_k": jnp.stack([two(k_norm_c[l]), two(_perm64(k_norm_c[l]))]).astype(F32),
        "g_a": row(g_out_a[l]), "g_b": row(g_out_b[l]), "g_c": row(g_out_c[l]),
        "w_out": w_out[l].astype(BF16), "g_ffn": row(ffn_norm[l]),
        "w_gate": w_gate[l].astype(BF16), "w_up": w_up[l].astype(BF16), "w_down": w_down[l].astype(BF16),
    }


def _rope_tables(S):
    tok = jnp.arange(S)
    t = tok.astype(F32)
    row = (tok // GRID_W).astype(F32)
    col = (tok % GRID_W).astype(F32)
    half = MLA_ROPE // 2
    inv = ROPE_BASE ** (-(jnp.arange(half, dtype=F32) * 2.0) / MLA_ROPE)

    def cs(pos):
        ang = pos[:, None] * inv[None, :]
        c, s = jnp.cos(ang), jnp.sin(ang)
        return jnp.concatenate([c, c], axis=1), jnp.concatenate([s, s], axis=1)

    ct, st = cs(t)
    ones = jnp.ones((S, MLA_NOPE), F32)
    zeros = jnp.zeros((S, MLA_NOPE), F32)
    pad1 = jnp.ones((S, LANES - MLA_QK), F32)
    pad0 = jnp.zeros((S, LANES - MLA_QK), F32)
    cr, sr = cs(row)
    cc, sc = cs(col)
    ca = jnp.concatenate([cr, cc], axis=1)
    sa = jnp.concatenate([sr, sc], axis=1)
    return {
        "cos_m": jnp.concatenate([ones, ct, pad1], axis=1),
        "sin_m": jnp.concatenate([zeros, st, pad0], axis=1),
        "cos_a": jnp.concatenate([ca, ca], axis=1),
        "sin_a": jnp.concatenate([sa, sa], axis=1),
    }


def _na_bias_table(rpb_l):
    c = np.arange(GRID_W)
    cstart = np.clip(c - NA_COLS // 2, 0, GRID_W - NA_COLS)
    cp = np.arange(GRID_W)
    dc = cp[None, :] - c[:, None] + (NA_COLS - 1)
    valid = (cp[None, :] >= cstart[:, None]) & (cp[None, :] < cstart[:, None] + NA_COLS)
    onehot = ((dc[:, :, None] == np.arange(2 * NA_COLS - 1)) & valid[:, :, None]).astype(np.float32)
    t = jnp.einsum("hrd,cpd->hrcp", rpb_l.astype(F32), onehot, precision=lax.Precision.HIGHEST)
    t = jnp.where(valid[None, None], t, MASK_VALUE)
    b = jnp.stack([t[:, NA_ROWS - 1 - d:2 * NA_ROWS - 1 - d] for d in range(NA_ROWS)], axis=1)
    b = b.transpose(0, 1, 3, 2, 4).reshape(NA_HEADS // 2, 2, NA_ROWS, GRID_W, NA_KEYS)
    return b.transpose(0, 2, 1, 3, 4).reshape(NA_HEADS // 2, NA_ROWS, 2 * GRID_W, NA_KEYS)


MLA_K_TILES = tuple(range(MLA_HEADS))
MLA_V_ROWS = tuple(range(MLA_HEADS))
GQA_K_TILES = (0,) * GQA_HEADS
GQA_V_ROWS = tuple(h // GQA_GROUP for h in range(GQA_HEADS))


def _trunk(x, layers, biases, tabs, g_final):
    n_layers = len(layers)
    for l, lw in enumerate(layers):
        na, mq, mk, mvt, gq, gk, gvt = _pre_call(x, lw, tabs)
        oa = _na_call(na, biases[l])
        ob = _attn_call(mq, mk, mvt, k_tiles=MLA_K_TILES, v_rows=MLA_V_ROWS, name="mla")
        oc = _attn_call(gq, gk, gvt, k_tiles=GQA_K_TILES, v_rows=GQA_V_ROWS, name="gqa")
        x = _post_call(x, oa, ob, oc, lw, g_final, final=(l == n_layers - 1))
    return x


def kernel(x_prompt, x_sample, attn_norm, w_in, q_a_norm, kv_a_norm, w_uq, w_ukv, q_norm_c, k_norm_c, rpb, g_out_a, g_out_b, g_out_c, w_out, ffn_norm, w_gate, w_up, w_down, final_norm):
    depth = w_in.shape[0]
    layers = [_layer_weights(l, attn_norm, w_in, q_a_norm, kv_a_norm, w_uq, w_ukv, q_norm_c,
                             k_norm_c, g_out_a, g_out_b, g_out_c, w_out, ffn_norm, w_gate,
                             w_up, w_down) for l in range(depth)]
    biases = [_na_bias_table(rpb[l]) for l in range(depth)]
    g_final = final_norm.reshape(1, -1).astype(F32)
    outs = []
    for x in (x_prompt, x_sample):
        tabs = _rope_tables(x.shape[1])
        outs.append(_trunk(x, layers, biases, tabs, g_final))
    return tuple(outs)
```

```python
import functools
import math

import jax
import jax.numpy as jnp
import numpy as np
from jax import lax
from jax.experimental import pallas as pl
from jax.experimental.pallas import tpu as pltpu

D_MODEL = 1024
GRID_W = 64
HEAD_DIM = 64
NA_HEADS = 4
NA_ROWS = 8
NA_COLS = 16
NA_D = NA_HEADS * HEAD_DIM
MLA_HEADS = 6
MLA_Q_RANK = 384
MLA_KV_RANK = 256
MLA_NOPE = 64
MLA_ROPE = 32
MLA_V = 64
MLA_QK = MLA_NOPE + MLA_ROPE
GQA_HEADS = 6
GQA_KV_HEADS = 2
GQA_GROUP = GQA_HEADS // GQA_KV_HEADS
D_FF = 2816
ROPE_BASE = 10000.0
EPS = 1e-6

LANES = 128
SUBLANES = 8
VMEM_LIMIT_BYTES = 56 * 1024 * 1024
ATTN_VMEM_BUDGET_BYTES = 48 * 1024 * 1024

PRE_TM = 512
ATTN_TQ = 512
ATTN_TK = PRE_TM
ATTN_TILES = 4
ATTN_UNROLL = 4
ONES_ROWS = 16
V_ROWS = HEAD_DIM + ONES_ROWS
NA_RB = 16
NA_KEYS = NA_ROWS * GRID_W
POST_TM = 512

LOG2E = 1.4426950408889634
MLA_QSCALE = LOG2E / math.sqrt(MLA_QK)
GQA_QSCALE = LOG2E / math.sqrt(HEAD_DIM)
NA_QSCALE = 1.0 / math.sqrt(HEAD_DIM)
MASK_VALUE = -1e30

BF16 = jnp.bfloat16
F32 = jnp.float32
NT_DIMS = (((1,), (1,)), ((), ()))


def _dot(a, b):
    return jnp.dot(a, b, preferred_element_type=F32)


def _dot_nt(a, b):
    return lax.dot_general(a, b, NT_DIMS, preferred_element_type=F32)


def _rms(x, g):
    return x * lax.rsqrt(jnp.mean(x * x, axis=-1, keepdims=True) + EPS) * g


def _with_ones_rows(vt, n_heads):
    ones = jnp.ones((ONES_ROWS, vt.shape[1]), F32)
    parts = []
    for h in range(n_heads):
        parts += [vt[h * HEAD_DIM:(h + 1) * HEAD_DIM], ones]
    return jnp.concatenate(parts, axis=0)


def _compiler_params(n_axes):
    return pltpu.CompilerParams(
        dimension_semantics=("arbitrary",) * n_axes,
        vmem_limit_bytes=VMEM_LIMIT_BYTES,
    )


def _const_spec(shape):
    nd = len(shape)
    return pl.BlockSpec(shape, lambda *_: (0,) * nd, pipeline_mode=pl.Buffered(1))


def _pre_kernel(x_ref, gattn_ref, wna_ref, wmla_ref, wqc_ref, wkc_ref, wvct_ref,
                qan_ref, kvan_ref, wuq_ref, wuk_ref, wuvt_ref, gq_ref, gk_ref,
                cosm_ref, sinm_ref, cosa_ref, sina_ref,
                na_ref, mq_ref, mk_ref, mvt_ref, gqo_ref, gko_ref, gvt_ref):
    x = x_ref[0]
    xb = _rms(x, gattn_ref[...]).astype(BF16)

    na_ref[0] = _dot(xb, wna_ref[...]).astype(BF16)

    cosm = cosm_ref[...]
    sinm = sinm_ref[...]
    cosa = cosa_ref[...]
    sina = sina_ref[...]

    lat = _dot(xb, wmla_ref[...])
    cq = lat[:, :MLA_Q_RANK]
    ckv = lat[:, MLA_Q_RANK:MLA_Q_RANK + MLA_KV_RANK]
    o = MLA_Q_RANK + MLA_KV_RANK
    kpe = lat[:, o:o + LANES] * cosm + lat[:, o + LANES:o + 2 * LANES] * sinm

    cqn = _rms(cq, qan_ref[...]).astype(BF16)
    q2 = _dot(cqn, wuq_ref[...])
    hw = MLA_HEADS * LANES
    for h in range(MLA_HEADS):
        sl = slice(h * LANES, (h + 1) * LANES)
        qh = q2[:, sl] * cosm + q2[:, hw + h * LANES:hw + (h + 1) * LANES] * sinm
        mq_ref[0, :, sl] = (qh * MLA_QSCALE).astype(BF16)

    ckvn = _rms(ckv, kvan_ref[...]).astype(BF16)
    kn = _dot(ckvn, wuk_ref[...])
    for h in range(MLA_HEADS):
        sl = slice(h * LANES, (h + 1) * LANES)
        mk_ref[0, :, sl] = (kn[:, sl] + kpe).astype(BF16)
    mvt_ref[0, 0] = _with_ones_rows(_dot_nt(wuvt_ref[...], ckvn), MLA_HEADS).astype(BF16)

    gq = gq_ref[...]
    gqc = cosa * gq[0:1]
    gqs = sina * gq[1:2]
    qc2 = _dot(xb, wqc_ref[...])
    hw = GQA_HEADS * LANES
    for h in range(GQA_HEADS):
        sl = slice(h * LANES, (h + 1) * LANES)
        xh = qc2[:, sl]
        xr = qc2[:, hw + h * LANES:hw + (h + 1) * LANES]
        ms = jnp.sum(xh * xh, axis=-1, keepdims=True) * (1.0 / HEAD_DIM)
        sc = lax.rsqrt(ms + EPS) * GQA_QSCALE
        gqo_ref[0, :, sl] = ((xh * gqc + xr * gqs) * sc).astype(BF16)

    gk = gk_ref[...]
    kc2 = _dot(xb, wkc_ref[...])
    xk = kc2[:, :LANES]
    xkr = kc2[:, LANES:]
    lane = lax.broadcasted_iota(jnp.int32, xk.shape, 1)
    first = lane < HEAD_DIM
    sq = xk * xk
    ms0 = jnp.sum(jnp.where(first, sq, 0.0), axis=-1, keepdims=True) * (1.0 / HEAD_DIM)
    ms1 = jnp.sum(jnp.where(first, 0.0, sq), axis=-1, keepdims=True) * (1.0 / HEAD_DIM)
    sc = jnp.where(first, lax.rsqrt(ms0 + EPS), lax.rsqrt(ms1 + EPS))
    gko_ref[0] = ((xk * (cosa * gk[0:1]) + xkr * (sina * gk[1:2])) * sc).astype(BF16)
    gvt_ref[0, 0] = _with_ones_rows(_dot_nt(wvct_ref[...], xb), GQA_KV_HEADS).astype(BF16)


def _pre_call(x, lw, tabs):
    B, S, _ = x.shape
    tm = PRE_TM
    nt = S // tm
    tile = lambda w: pl.BlockSpec((1, tm, w), lambda b, i: (b, i, 0))
    tab = pl.BlockSpec((tm, LANES), lambda b, i: (i, 0))
    weights = [lw["g_attn"], lw["w_na"], lw["w_mla"], lw["w_qc"], lw["w_kc"], lw["w_vct"],
               lw["g_qa"], lw["g_kva"], lw["w_uq"], lw["w_uk"], lw["w_uvt"], lw["g_q"], lw["g_k"]]
    out_shape = (
        jax.ShapeDtypeStruct((B, S, 3 * NA_D), BF16),
        jax.ShapeDtypeStruct((B, S, MLA_HEADS * LANES), BF16),
        jax.ShapeDtypeStruct((B, S, MLA_HEADS * LANES), BF16),
        jax.ShapeDtypeStruct((B, nt, MLA_HEADS * V_ROWS, tm), BF16),
        jax.ShapeDtypeStruct((B, S, GQA_HEADS * LANES), BF16),
        jax.ShapeDtypeStruct((B, S, GQA_KV_HEADS * HEAD_DIM), BF16),
        jax.ShapeDtypeStruct((B, nt, GQA_KV_HEADS * V_ROWS, tm), BF16),
    )
    out_specs = (
        tile(3 * NA_D), tile(MLA_HEADS * LANES), tile(MLA_HEADS * LANES),
        pl.BlockSpec((1, 1, MLA_HEADS * V_ROWS, tm), lambda b, i: (b, i, 0, 0)),
        tile(GQA_HEADS * LANES), tile(GQA_KV_HEADS * HEAD_DIM),
        pl.BlockSpec((1, 1, GQA_KV_HEADS * V_ROWS, tm), lambda b, i: (b, i, 0, 0)),
    )
    return pl.pallas_call(
        _pre_kernel,
        grid=(B, nt),
        in_specs=[tile(D_MODEL)] + [_const_spec(w.shape) for w in weights] + [tab] * 4,
        out_specs=out_specs,
        out_shape=out_shape,
        compiler_params=_compiler_params(2),
        name="pre",
    )(x, *weights, tabs["cos_m"], tabs["sin_m"], tabs["cos_a"], tabs["sin_a"])


def _column_max(x):
    rows = x.shape[0]
    while rows > SUBLANES:
        rows //= 2
        x = jnp.maximum(x[:rows], x[rows:])
    return jnp.max(x, axis=0, keepdims=True)


def _attn_kernel(q_ref, k_ref, vt_ref, o_ref, m_ref, acc_ref, s_ref, cm_ref, *, n_heads,
                 k_tiles, v_rows, n_chunks, n_tiles, tq, tk):
    def reset():
        m_ref[...] = jnp.full(m_ref.shape, -jnp.inf, F32)
        acc_ref[...] = jnp.zeros(acc_ref.shape, F32)

    def scores(t, j, slot, h):
        row0 = pl.multiple_of(t * tq, tq)
        start = pl.multiple_of(j * tk, tk)
        kt = k_tiles[h]
        qh = q_ref[0, pl.ds(row0, tq), h * LANES:(h + 1) * LANES]
        kc = k_ref[0, pl.ds(start, tk), kt * LANES:(kt + 1) * LANES]
        st = _dot_nt(kc, qh)
        s_ref[slot, h] = st
        cm_ref[slot, h] = _column_max(st)

    def accumulate(j, slot, h):
        vr = v_rows[h]
        m_old = m_ref[h]
        m_new = jnp.maximum(m_old, cm_ref[slot, h])
        alpha = jnp.exp2(m_old - m_new)
        p = jnp.exp2(s_ref[slot, h] - m_new).astype(BF16)
        vc = vt_ref[0, j, vr * V_ROWS:(vr + 1) * V_ROWS, :]
        acc_ref[h] = alpha * acc_ref[h] + _dot(vc, p)
        m_ref[h] = m_new

    def step(t, j, slot, t_next, j_next):
        for h in range(n_heads):
            scores(t_next, j_next, 1 - slot, h)
            accumulate(j, slot, h)

    def tile_body(t, carry):
        def chunks_body(jb, c):
            j0 = jb * ATTN_UNROLL
            for u in range(ATTN_UNROLL - 1):
                step(t, j0 + u, u % 2, t, j0 + u + 1)
            j_last = j0 + ATTN_UNROLL - 1
            wrap = (j_last + 1 == n_chunks).astype(jnp.int32)
            step(t, j_last, (ATTN_UNROLL - 1) % 2,
                 jnp.minimum(t + wrap, n_tiles - 1), (1 - wrap) * (j_last + 1))
            return c

        lax.fori_loop(0, n_chunks // ATTN_UNROLL, chunks_body, 0)
        outs = []
        for h in range(n_heads):
            acc = acc_ref[h]
            outs.append(acc[:HEAD_DIM] / acc[HEAD_DIM:HEAD_DIM + 1])
        o_ref[0, pl.ds(pl.multiple_of(t * tq, tq), tq), :] = jnp.concatenate(outs, axis=0).T
        reset()
        return carry

    reset()
    for h in range(n_heads):
        scores(0, 0, 0, h)
    lax.fori_loop(0, n_tiles, tile_body, 0)


def _attn_call(q, k, vt, *, k_tiles, v_rows, name):
    B, S, qw = q.shape
    n_heads = qw // LANES
    tq, tk, n_tiles = ATTN_TQ, ATTN_TK, ATTN_TILES
    n_chunks = S // tk
    rows = tq * n_tiles
    assert S % rows == 0 and S % tk == 0 and n_chunks % ATTN_UNROLL == 0, (S, rows, tk)
    assert ATTN_UNROLL % 2 == 0
    kw = k.shape[2]
    vw = vt.shape[2]
    ow = n_heads * HEAD_DIM
    kv_bytes = S * (kw + vw) * 2
    other_bytes = 2 * rows * (qw * 2 + ow * 4) + 2 * n_heads * tk * tq * 4
    kv_buffers = 2 if 2 * kv_bytes + other_bytes < ATTN_VMEM_BUDGET_BYTES else 1
    kern = functools.partial(_attn_kernel, n_heads=n_heads, k_tiles=k_tiles, v_rows=v_rows,
                             n_chunks=n_chunks, n_tiles=n_tiles, tq=tq, tk=tk)
    return pl.pallas_call(
        kern,
        grid=(B, S // rows),
        in_specs=[
            pl.BlockSpec((1, rows, qw), lambda b, i: (b, i, 0)),
            pl.BlockSpec((1, S, kw), lambda b, i: (b, 0, 0), pipeline_mode=pl.Buffered(kv_buffers)),
            pl.BlockSpec((1, n_chunks, vw, tk), lambda b, i: (b, 0, 0, 0),
                         pipeline_mode=pl.Buffered(kv_buffers)),
        ],
        out_specs=pl.BlockSpec((1, rows, ow), lambda b, i: (b, i, 0)),
        out_shape=jax.ShapeDtypeStruct((B, S, ow), F32),
        scratch_shapes=[pltpu.VMEM((n_heads, 1, tq), F32),
                        pltpu.VMEM((n_heads, V_ROWS, tq), F32),
                        pltpu.VMEM((2, n_heads, tk, tq), F32),
                        pltpu.VMEM((2, n_heads, 1, tq), F32)],
        compiler_params=_compiler_params(2),
        name=name,
    )(q, k, vt)


def _na_kernel(na_ref, bias_ref, o_ref, s_ref, *, rows, rb):
    i = pl.program_id(1)
    lane = lax.broadcasted_iota(jnp.int32, (GRID_W, LANES), 1)
    first = lane < HEAD_DIM
    n_pairs = NA_HEADS // 2

    def window_start(u):
        r = i * rb + u
        rs = jnp.clip(r - NA_ROWS // 2, 0, rows - NA_ROWS)
        return r, rs

    def scores(u, t):
        r, rs = window_start(u)
        kv0 = pl.multiple_of(rs * GRID_W, GRID_W)
        qt = na_ref[0, pl.ds(pl.multiple_of(r * GRID_W, GRID_W), GRID_W), t * LANES:(t + 1) * LANES]
        kt = na_ref[0, pl.ds(kv0, NA_KEYS), NA_D + t * LANES:NA_D + (t + 1) * LANES]
        zero = jnp.zeros_like(qt)
        qm = jnp.concatenate([jnp.where(first, qt, zero), jnp.where(first, zero, qt)], axis=0)
        s_ref[u % 2, t] = _dot_nt(qm, kt) + bias_ref[t, r - rs]

    def finish(u, t):
        _, rs = window_start(u)
        kv0 = pl.multiple_of(rs * GRID_W, GRID_W)
        vt = na_ref[0, pl.ds(kv0, NA_KEYS), 2 * NA_D + t * LANES:2 * NA_D + (t + 1) * LANES]
        s = s_ref[u % 2, t]
        m = jnp.max(s, axis=-1, keepdims=True)
        p = jnp.exp(s - m)
        l = jnp.sum(p, axis=-1, keepdims=True)
        o2 = _dot(p.astype(BF16), vt) / l
        o_ref[0, u * GRID_W:(u + 1) * GRID_W, t * LANES:(t + 1) * LANES] = (
            jnp.where(first, o2[:GRID_W], o2[GRID_W:]))

    for t in range(n_pairs):
        scores(0, t)
    for u in range(rb):
        for t in range(n_pairs):
            finish(u, t)
            if u + 1 < rb:
                scores(u + 1, t)


def _na_call(na, bias):
    B, S, w = na.shape
    rows = S // GRID_W
    rb = NA_RB
    kern = functools.partial(_na_kernel, rows=rows, rb=rb)
    return pl.pallas_call(
        kern,
        grid=(B, rows // rb),
        in_specs=[
            pl.BlockSpec((1, S, w), lambda b, i: (b, 0, 0)),
            _const_spec(bias.shape),
        ],
        out_specs=pl.BlockSpec((1, rb * GRID_W, NA_D), lambda b, i: (b, i, 0)),
        out_shape=jax.ShapeDtypeStruct((B, S, NA_D), F32),
        scratch_shapes=[pltpu.VMEM((2, NA_HEADS // 2, 2 * GRID_W, NA_KEYS), F32)],
        compiler_params=_compiler_params(2),
        name="na",
    )(na, bias)


def _post_kernel(x_ref, oa_ref, ob_ref, oc_ref, ga_ref, gb_ref, gc_ref, wout_ref, gffn_ref,
                 wg_ref, wu_ref, wd_ref, gfin_ref, o_ref, *, final):
    merged = jnp.concatenate(
        [_rms(oa_ref[...], ga_ref[...]), _rms(ob_ref[...], gb_ref[...]),
         _rms(oc_ref[...], gc_ref[...])], axis=-1).astype(BF16)
    x1 = x_ref[...] + _dot(merged, wout_ref[...])
    hb = _rms(x1, gffn_ref[...]).astype(BF16)
    g = _dot(hb, wg_ref[...])
    u = _dot(hb, wu_ref[...])
    act = (g * jax.nn.sigmoid(g) * u).astype(BF16)
    x2 = x1 + _dot(act, wd_ref[...])
    if final:
        x2 = _rms(x2, gfin_ref[...])
    o_ref[...] = x2


def _post_call(x, oa, ob, oc, lw, g_final, final):
    B, S, _ = x.shape
    T = B * S
    tm = POST_TM
    flat = lambda a: a.reshape(T, a.shape[-1])
    tile = lambda w: pl.BlockSpec((tm, w), lambda i: (i, 0))
    weights_a = [lw["g_a"], lw["g_b"], lw["g_c"], lw["w_out"], lw["g_ffn"],
                 lw["w_gate"], lw["w_up"], lw["w_down"], g_final]
    out = pl.pallas_call(
        functools.partial(_post_kernel, final=final),
        grid=(T // tm,),
        in_specs=[tile(D_MODEL), tile(NA_D), tile(MLA_HEADS * MLA_V), tile(GQA_HEADS * HEAD_DIM)]
        + [_const_spec(w.shape) for w in weights_a],
        out_specs=tile(D_MODEL),
        out_shape=jax.ShapeDtypeStruct((T, D_MODEL), F32),
        compiler_params=_compiler_params(1),
        name="post",
    )(flat(x), flat(oa), flat(ob), flat(oc), *weights_a)
    return out.reshape(B, S, D_MODEL)


def _rot32(w):
    return jnp.concatenate([-w[:, 16:32], w[:, 0:16]], axis=1)


def _rot64(w):
    return jnp.concatenate([_rot32(w[:, :32]), _rot32(w[:, 32:])], axis=1)


def _perm64(g):
    p = lambda v: jnp.concatenate([v[16:32], v[0:16]])
    return jnp.concatenate([p(g[:32]), p(g[32:])])


def _layer_weights(l, attn_norm, w_in, q_a_norm, kv_a_norm, w_uq, w_ukv, q_norm_c, k_norm_c,
                   g_out_a, g_out_b, g_out_c, w_out, ffn_norm, w_gate, w_up, w_down):
    wi = w_in[l]
    K = wi.shape[0]
    sizes = (NA_D, NA_D, NA_D, MLA_Q_RANK, MLA_KV_RANK, MLA_ROPE,
             GQA_HEADS * HEAD_DIM, GQA_KV_HEADS * HEAD_DIM, GQA_KV_HEADS * HEAD_DIM)
    offs = np.concatenate([[0], np.cumsum(sizes)])
    (wqa, wka, wva, wcq, wckv, wkpe, wqc, wkc, wvc) = [wi[:, offs[i]:offs[i + 1]] for i in range(9)]
    z = lambda n: jnp.zeros((K, n), F32)

    w_na = jnp.concatenate([wqa * NA_QSCALE, wka, wva], axis=1)
    kpe_tile = jnp.concatenate([z(MLA_NOPE), wkpe, z(LANES - MLA_QK)], axis=1)
    kpe_rot = jnp.concatenate([z(MLA_NOPE), _rot32(wkpe), z(LANES - MLA_QK)], axis=1)
    w_mla = jnp.concatenate([wcq, wckv, kpe_tile, kpe_rot], axis=1)

    qc_tiles, qc_rots = [], []
    for h in range(GQA_HEADS):
        wh = wqc[:, h * HEAD_DIM:(h + 1) * HEAD_DIM]
        pad = (lambda a: jnp.concatenate([a, z(HEAD_DIM)], axis=1)) if h // GQA_GROUP == 0 else \
              (lambda a: jnp.concatenate([z(HEAD_DIM), a], axis=1))
        qc_tiles.append(pad(wh))
        qc_rots.append(pad(_rot64(wh)))
    w_qc = jnp.concatenate(qc_tiles + qc_rots, axis=1)
    w_kc = jnp.concatenate(
        [wkc] + [_rot64(wkc[:, h * HEAD_DIM:(h + 1) * HEAD_DIM]) for h in range(GQA_KV_HEADS)], axis=1)

    uq = w_uq[l]
    zq = lambda n: jnp.zeros((MLA_Q_RANK, n), F32)
    uq_tiles, uq_rots = [], []
    for h in range(MLA_HEADS):
        wh = uq[:, h * MLA_QK:(h + 1) * MLA_QK]
        uq_tiles.append(jnp.concatenate([wh, zq(LANES - MLA_QK)], axis=1))
        uq_rots.append(jnp.concatenate([zq(MLA_NOPE), _rot32(wh[:, MLA_NOPE:]), zq(LANES - MLA_QK)], axis=1))
    w_uq_ext = jnp.concatenate(uq_tiles + uq_rots, axis=1)

    ukv = w_ukv[l]
    zk = jnp.zeros((MLA_KV_RANK, LANES - MLA_NOPE), F32)
    per = MLA_NOPE + MLA_V
    w_uk = jnp.concatenate(
        [jnp.concatenate([ukv[:, h * per:h * per + MLA_NOPE], zk], axis=1) for h in range(MLA_HEADS)], axis=1)
    w_uvt = jnp.concatenate([ukv[:, h * per + MLA_NOPE:(h + 1) * per] for h in range(MLA_HEADS)], axis=1).T

    two = lambda g: jnp.concatenate([g, g])
    row = lambda g: g.reshape(1, -1).astype(F32)
    return {
        "g_attn": row(attn_norm[l]),
        "w_na": w_na.astype(BF16), "w_mla": w_mla.astype(BF16), "w_qc": w_qc.astype(BF16),
        "w_kc": w_kc.astype(BF16), "w_vct": wvc.T.astype(BF16),
        "g_qa": row(q_a_norm[l]), "g_kva": row(kv_a_norm[l]),
        "w_uq": w_uq_ext.astype(BF16), "w_uk": w_uk.astype(BF16), "w_uvt": w_uvt.astype(BF16),
        "g_q": jnp.stack([two(q_norm_c[l]), two(_perm64(q_norm_c[l]))]).astype(F32),
        "g_k": jnp.stack([two(k_norm_c[l]), two(_perm64(k_norm_c[l]))]).astype(F32),
        "g_a": row(g_out_a[l]), "g_b": row(g_out_b[l]), "g_c": row(g_out_c[l]),
        "w_out": w_out[l].astype(BF16), "g_ffn": row(ffn_norm[l]),
        "w_gate": w_gate[l].astype(BF16), "w_up": w_up[l].astype(BF16), "w_down": w_down[l].astype(BF16),
    }


def _rope_tables(S):
    tok = jnp.arange(S)
    t = tok.astype(F32)
    row = (tok // GRID_W).astype(F32)
    col = (tok % GRID_W).astype(F32)
    half = MLA_ROPE // 2
    inv = ROPE_BASE ** (-(jnp.arange(half, dtype=F32) * 2.0) / MLA_ROPE)

    def cs(pos):
        ang = pos[:, None] * inv[None, :]
        c, s = jnp.cos(ang), jnp.sin(ang)
        return jnp.concatenate([c, c], axis=1), jnp.concatenate([s, s], axis=1)

    ct, st = cs(t)
    ones = jnp.ones((S, MLA_NOPE), F32)
    zeros = jnp.zeros((S, MLA_NOPE), F32)
    pad1 = jnp.ones((S, LANES - MLA_QK), F32)
    pad0 = jnp.zeros((S, LANES - MLA_QK), F32)
    cr, sr = cs(row)
    cc, sc = cs(col)
    ca = jnp.concatenate([cr, cc], axis=1)
    sa = jnp.concatenate([sr, sc], axis=1)
    return {
        "cos_m": jnp.concatenate([ones, ct, pad1], axis=1),
        "sin_m": jnp.concatenate([zeros, st, pad0], axis=1),
        "cos_a": jnp.concatenate([ca, ca], axis=1),
        "sin_a": jnp.concatenate([sa, sa], axis=1),
    }


def _na_bias_table(rpb_l):
    c = np.arange(GRID_W)
    cstart = np.clip(c - NA_COLS // 2, 0, GRID_W - NA_COLS)
    cp = np.arange(GRID_W)
    dc = cp[None, :] - c[:, None] + (NA_COLS - 1)
    valid = (cp[None, :] >= cstart[:, None]) & (cp[None, :] < cstart[:, None] + NA_COLS)
    onehot = ((dc[:, :, None] == np.arange(2 * NA_COLS - 1)) & valid[:, :, None]).astype(np.float32)
    t = jnp.einsum("hrd,cpd->hrcp", rpb_l.astype(F32), onehot, precision=lax.Precision.HIGHEST)
    t = jnp.where(valid[None, None], t, MASK_VALUE)
    b = jnp.stack([t[:, NA_ROWS - 1 - d:2 * NA_ROWS - 1 - d] for d in range(NA_ROWS)], axis=1)
    b = b.transpose(0, 1, 3, 2, 4).reshape(NA_HEADS // 2, 2, NA_ROWS, GRID_W, NA_KEYS)
    return b.transpose(0, 2, 1, 3, 4).reshape(NA_HEADS // 2, NA_ROWS, 2 * GRID_W, NA_KEYS)


MLA_K_TILES = tuple(range(MLA_HEADS))
MLA_V_ROWS = tuple(range(MLA_HEADS))
GQA_K_TILES = (0,) * GQA_HEADS
GQA_V_ROWS = tuple(h // GQA_GROUP for h in range(GQA_HEADS))


def _trunk(x, layers, biases, tabs, g_final):
    n_layers = len(layers)
    for l, lw in enumerate(layers):
        na, mq, mk, mvt, gq, gk, gvt = _pre_call(x, lw, tabs)
        oa = _na_call(na, biases[l])
        ob = _attn_call(mq, mk, mvt, k_tiles=MLA_K_TILES, v_rows=MLA_V_ROWS, name="mla")
        oc = _attn_call(gq, gk, gvt, k_tiles=GQA_K_TILES, v_rows=GQA_V_ROWS, name="gqa")
        x = _post_call(x, oa, ob, oc, lw, g_final, final=(l == n_layers - 1))
    return x


def kernel(x_prompt, x_sample, attn_norm, w_in, q_a_norm, kv_a_norm, w_uq, w_ukv, q_norm_c, k_norm_c, rpb, g_out_a, g_out_b, g_out_c, w_out, ffn_norm, w_gate, w_up, w_down, final_norm):
    depth = w_in.shape[0]
    layers = [_layer_weights(l, attn_norm, w_in, q_a_norm, kv_a_norm, w_uq, w_ukv, q_norm_c,
                             k_norm_c, g_out_a, g_out_b, g_out_c, w_out, ffn_norm, w_gate,
                             w_up, w_down) for l in range(depth)]
    biases = [_na_bias_table(rpb[l]) for l in range(depth)]
    g_final = final_norm.reshape(1, -1).astype(F32)
    outs = []
    for x in (x_prompt, x_sample):
        tabs = _rope_tables(x.shape[1])
        outs.append(_trunk(x, layers, biases, tabs, g_final))
    return tuple(outs)
```

```python
import functools
import math

import jax
import jax.numpy as jnp
import numpy as np
from jax import lax
from jax.experimental import pallas as pl
from jax.experimental.pallas import tpu as pltpu

D_MODEL = 1024
GRID_W = 64
HEAD_DIM = 64
NA_HEADS = 4
NA_ROWS = 8
NA_COLS = 16
NA_D = NA_HEADS * HEAD_DIM
MLA_HEADS = 6
MLA_Q_RANK = 384
MLA_KV_RANK = 256
MLA_NOPE = 64
MLA_ROPE = 32
MLA_V = 64
MLA_QK = MLA_NOPE + MLA_ROPE
GQA_HEADS = 6
GQA_KV_HEADS = 2
GQA_GROUP = GQA_HEADS // GQA_KV_HEADS
D_FF = 2816
ROPE_BASE = 10000.0
EPS = 1e-6

LANES = 128
SUBLANES = 8
VMEM_LIMIT_BYTES = 56 * 1024 * 1024
ATTN_VMEM_BUDGET_BYTES = 48 * 1024 * 1024

PRE_TM = 512
ATTN_TQ = 512
ATTN_TK = PRE_TM
ATTN_TILES = 4
ATTN_UNROLL = 4
ONES_ROWS = 16
V_ROWS = HEAD_DIM + ONES_ROWS
NA_RB = 16
NA_KEYS = NA_ROWS * GRID_W
POST_TM = 512

LOG2E = 1.4426950408889634
MLA_QSCALE = LOG2E / math.sqrt(MLA_QK)
GQA_QSCALE = LOG2E / math.sqrt(HEAD_DIM)
NA_QSCALE = 1.0 / math.sqrt(HEAD_DIM)
MASK_VALUE = -1e30

BF16 = jnp.bfloat16
F32 = jnp.float32
NT_DIMS = (((1,), (1,)), ((), ()))


def _dot(a, b):
    return jnp.dot(a, b, preferred_element_type=F32)


def _dot_nt(a, b):
    return lax.dot_general(a, b, NT_DIMS, preferred_element_type=F32)


def _rms(x, g):
    return x * lax.rsqrt(jnp.mean(x * x, axis=-1, keepdims=True) + EPS) * g


def _with_ones_rows(vt, n_heads):
    ones = jnp.ones((ONES_ROWS, vt.shape[1]), F32)
    parts = []
    for h in range(n_heads):
        parts += [vt[h * HEAD_DIM:(h + 1) * HEAD_DIM], ones]
    return jnp.concatenate(parts, axis=0)


def _compiler_params(n_axes):
    return pltpu.CompilerParams(
        dimension_semantics=("arbitrary",) * n_axes,
        vmem_limit_bytes=VMEM_LIMIT_BYTES,
    )


def _const_spec(shape):
    nd = len(shape)
    return pl.BlockSpec(shape, lambda *_: (0,) * nd, pipeline_mode=pl.Buffered(1))


def _pre_kernel(x_ref, gattn_ref, wna_ref, wmla_ref, wqc_ref, wkc_ref, wvct_ref,
                qan_ref, kvan_ref, wuq_ref, wuk_ref, wuvt_ref, gq_ref, gk_ref,
                cosm_ref, sinm_ref, cosa_ref, sina_ref,
                na_ref, mq_ref, mk_ref, mvt_ref, gqo_ref, gko_ref, gvt_ref):
    x = x_ref[0]
    xb = _rms(x, gattn_ref[...]).astype(BF16)

    na_ref[0] = _dot(xb, wna_ref[...]).astype(BF16)

    cosm = cosm_ref[...]
    sinm = sinm_ref[...]
    cosa = cosa_ref[...]
    sina = sina_ref[...]

    lat = _dot(xb, wmla_ref[...])
    cq = lat[:, :MLA_Q_RANK]
    ckv = lat[:, MLA_Q_RANK:MLA_Q_RANK + MLA_KV_RANK]
    o = MLA_Q_RANK + MLA_KV_RANK
    kpe = lat[:, o:o + LANES] * cosm + lat[:, o + LANES:o + 2 * LANES] * sinm

    cqn = _rms(cq, qan_ref[...]).astype(BF16)
    q2 = _dot(cqn, wuq_ref[...])
    hw = MLA_HEADS * LANES
    for h in range(MLA_HEADS):
        sl = slice(h * LANES, (h + 1) * LANES)
        qh = q2[:, sl] * cosm + q2[:, hw + h * LANES:hw + (h + 1) * LANES] * sinm
        mq_ref[0, :, sl] = (qh * MLA_QSCALE).astype(BF16)

    ckvn = _rms(ckv, kvan_ref[...]).astype(BF16)
    kn = _dot(ckvn, wuk_ref[...])
    for h in range(MLA_HEADS):
        sl = slice(h * LANES, (h + 1) * LANES)
        mk_ref[0, :, sl] = (kn[:, sl] + kpe).astype(BF16)
    mvt_ref[0, 0] = _with_ones_rows(_dot_nt(wuvt_ref[...], ckvn), MLA_HEADS).astype(BF16)

    gq = gq_ref[...]
    gqc = cosa * gq[0:1]
    gqs = sina * gq[1:2]
    qc2 = _dot(xb, wqc_ref[...])
    hw = GQA_HEADS * LANES
    for h in range(GQA_HEADS):
        sl = slice(h * LANES, (h + 1) * LANES)
        xh = qc2[:, sl]
        xr = qc2[:, hw + h * LANES:hw + (h + 1) * LANES]
        ms = jnp.sum(xh * xh, axis=-1, keepdims=True) * (1.0 / HEAD_DIM)
        sc = lax.rsqrt(ms + EPS) * GQA_QSCALE
        gqo_ref[0, :, sl] = ((xh * gqc + xr * gqs) * sc).astype(BF16)

    gk = gk_ref[...]
    kc2 = _dot(xb, wkc_ref[...])
    xk = kc2[:, :LANES]
    xkr = kc2[:, LANES:]
    lane = lax.broadcasted_iota(jnp.int32, xk.shape, 1)
    first = lane < HEAD_DIM
    sq = xk * xk
    ms0 = jnp.sum(jnp.where(first, sq, 0.0), axis=-1, keepdims=True) * (1.0 / HEAD_DIM)
    ms1 = jnp.sum(jnp.where(first, 0.0, sq), axis=-1, keepdims=True) * (1.0 / HEAD_DIM)
    sc = jnp.where(first, lax.rsqrt(ms0 + EPS), lax.rsqrt(ms1 + EPS))
    gko_ref[0] = ((xk * (cosa * gk[0:1]) + xkr * (sina * gk[1:2])) * sc).astype(BF16)
    gvt_ref[0, 0] = _with_ones_rows(_dot_nt(wvct_ref[...], xb), GQA_KV_HEADS).astype(BF16)


def _pre_call(x, lw, tabs):
    B, S, _ = x.shape
    tm = PRE_TM
    nt = S // tm
    tile = lambda w: pl.BlockSpec((1, tm, w), lambda b, i: (b, i, 0))
    tab = pl.BlockSpec((tm, LANES), lambda b, i: (i, 0))
    weights = [lw["g_attn"], lw["w_na"], lw["w_mla"], lw["w_qc"], lw["w_kc"], lw["w_vct"],
               lw["g_qa"], lw["g_kva"], lw["w_uq"], lw["w_uk"], lw["w_uvt"], lw["g_q"], lw["g_k"]]
    out_shape = (
        jax.ShapeDtypeStruct((B, S, 3 * NA_D), BF16),
        jax.ShapeDtypeStruct((B, S, MLA_HEADS * LANES), BF16),
        jax.ShapeDtypeStruct((B, S, MLA_HEADS * LANES), BF16),
        jax.ShapeDtypeStruct((B, nt, MLA_HEADS * V_ROWS, tm), BF16),
        jax.ShapeDtypeStruct((B, S, GQA_HEADS * LANES), BF16),
        jax.ShapeDtypeStruct((B, S, GQA_KV_HEADS * HEAD_DIM), BF16),
        jax.ShapeDtypeStruct((B, nt, GQA_KV_HEADS * V_ROWS, tm), BF16),
    )
    out_specs = (
        tile(3 * NA_D), tile(MLA_HEADS * LANES), tile(MLA_HEADS * LANES),
        pl.BlockSpec((1, 1, MLA_HEADS * V_ROWS, tm), lambda b, i: (b, i, 0, 0)),
        tile(GQA_HEADS * LANES), tile(GQA_KV_HEADS * HEAD_DIM),
        pl.BlockSpec((1, 1, GQA_KV_HEADS * V_ROWS, tm), lambda b, i: (b, i, 0, 0)),
    )
    return pl.pallas_call(
        _pre_kernel,
        grid=(B, nt),
        in_specs=[tile(D_MODEL)] + [_const_spec(w.shape) for w in weights] + [tab] * 4,
        out_specs=out_specs,
        out_shape=out_shape,
        compiler_params=_compiler_params(2),
        name="pre",
    )(x, *weights, tabs["cos_m"], tabs["sin_m"], tabs["cos_a"], tabs["sin_a"])


def _column_max(x):
    rows = x.shape[0]
    while rows > SUBLANES:
        rows //= 2
        x = jnp.maximum(x[:rows], x[rows:])
    return jnp.max(x, axis=0, keepdims=True)


def _attn_kernel(q_ref, k_ref, vt_ref, o_ref, m_ref, acc_ref, s_ref, cm_ref, *, n_heads,
                 k_tiles, v_rows, n_chunks, n_tiles, tq, tk):
    def reset():
        m_ref[...] = jnp.full(m_ref.shape, -jnp.inf, F32)
        acc_ref[...] = jnp.zeros(acc_ref.shape, F32)

    def scores(t, j, slot, h):
        row0 = pl.multiple_of(t * tq, tq)
        start = pl.multiple_of(j * tk, tk)
        kt = k_tiles[h]
        qh = q_ref[0, pl.ds(row0, tq), h * LANES:(h + 1) * LANES]
        kc = k_ref[0, pl.ds(start, tk), kt * LANES:(kt + 1) * LANES]
        st = _dot_nt(kc, qh)
        s_ref[slot, h] = st
        cm_ref[slot, h] = _column_max(st)

    def accumulate(j, slot, h):
        vr = v_rows[h]
        m_old = m_ref[h]
        m_new = jnp.maximum(m_old, cm_ref[slot, h])
        alpha = jnp.exp2(m_old - m_new)
        p = jnp.exp2(s_ref[slot, h] - m_new).astype(BF16)
        vc = vt_ref[0, j, vr * V_ROWS:(vr + 1) * V_ROWS, :]
        acc_ref[h] = alpha * acc_ref[h] + _dot(vc, p)
        m_ref[h] = m_new

    def step(t, j, slot, t_next, j_next):
        for h in range(n_heads):
            scores(t_next, j_next, 1 - slot, h)
            accumulate(j, slot, h)

    def tile_body(t, carry):
        def chunks_body(jb, c):
            j0 = jb * ATTN_UNROLL
            for u in range(ATTN_UNROLL - 1):
                step(t, j0 + u, u % 2, t, j0 + u + 1)
            j_last = j0 + ATTN_UNROLL - 1
            wrap = (j_last + 1 == n_chunks).astype(jnp.int32)
            step(t, j_last, (ATTN_UNROLL - 1) % 2,
                 jnp.minimum(t + wrap, n_tiles - 1), (1 - wrap) * (j_last + 1))
            return c

        lax.fori_loop(0, n_chunks // ATTN_UNROLL, chunks_body, 0)
        outs = []
        for h in range(n_heads):
            acc = acc_ref[h]
            outs.append(acc[:HEAD_DIM] / acc[HEAD_DIM:HEAD_DIM + 1])
        o_ref[0, pl.ds(pl.multiple_of(t * tq, tq), tq), :] = jnp.concatenate(outs, axis=0).T
        reset()
        return carry

    reset()
    for h in range(n_heads):
        scores(0, 0, 0, h)
    lax.fori_loop(0, n_tiles, tile_body, 0)


def _attn_call(q, k, vt, *, k_tiles, v_rows, name):
    B, S, qw = q.shape
    n_heads = qw // LANES
    tq, tk, n_tiles = ATTN_TQ, ATTN_TK, ATTN_TILES
    n_chunks = S // tk
    rows = tq * n_tiles
    assert S % rows == 0 and S % tk == 0 and n_chunks % ATTN_UNROLL == 0, (S, rows, tk)
    assert ATTN_UNROLL % 2 == 0
    kw = k.shape[2]
    vw = vt.shape[2]
    ow = n_heads * HEAD_DIM
    kv_bytes = S * (kw + vw) * 2
    other_bytes = 2 * rows * (qw * 2 + ow * 4) + 2 * n_heads * tk * tq * 4
    kv_buffers = 2 if 2 * kv_bytes + other_bytes < ATTN_VMEM_BUDGET_BYTES else 1
    kern = functools.partial(_attn_kernel, n_heads=n_heads, k_tiles=k_tiles, v_rows=v_rows,
                             n_chunks=n_chunks, n_tiles=n_tiles, tq=tq, tk=tk)
    return pl.pallas_call(
        kern,
        grid=(B, S // rows),
        in_specs=[
            pl.BlockSpec((1, rows, qw), lambda b, i: (b, i, 0)),
            pl.BlockSpec((1, S, kw), lambda b, i: (b, 0, 0), pipeline_mode=pl.Buffered(kv_buffers)),
            pl.BlockSpec((1, n_chunks, vw, tk), lambda b, i: (b, 0, 0, 0),
                         pipeline_mode=pl.Buffered(kv_buffers)),
        ],
        out_specs=pl.BlockSpec((1, rows, ow), lambda b, i: (b, i, 0)),
        out_shape=jax.ShapeDtypeStruct((B, S, ow), F32),
        scratch_shapes=[pltpu.VMEM((n_heads, 1, tq), F32),
                        pltpu.VMEM((n_heads, V_ROWS, tq), F32),
                        pltpu.VMEM((2, n_heads, tk, tq), F32),
                        pltpu.VMEM((2, n_heads, 1, tq), F32)],
        compiler_params=_compiler_params(2),
        name=name,
    )(q, k, vt)


def _na_kernel(na_ref, bias_ref, o_ref, s_ref, *, rows, rb):
    i = pl.program_id(1)
    lane = lax.broadcasted_iota(jnp.int32, (GRID_W, LANES), 1)
    first = lane < HEAD_DIM
    n_pairs = NA_HEADS // 2

    def window_start(u):
        r = i * rb + u
        rs = jnp.clip(r - NA_ROWS // 2, 0, rows - NA_ROWS)
        return r, rs

    def scores(u, t):
        r, rs = window_start(u)
        kv0 = pl.multiple_of(rs * GRID_W, GRID_W)
        qt = na_ref[0, pl.ds(pl.multiple_of(r * GRID_W, GRID_W), GRID_W), t * LANES:(t + 1) * LANES]
        kt = na_ref[0, pl.ds(kv0, NA_KEYS), NA_D + t * LANES:NA_D + (t + 1) * LANES]
        zero = jnp.zeros_like(qt)
        qm = jnp.concatenate([jnp.where(first, qt, zero), jnp.where(first, zero, qt)], axis=0)
        s_ref[u % 2, t] = _dot_nt(qm, kt) + bias_ref[t, r - rs]

    def finish(u, t):
        _, rs = window_start(u)
        kv0 = pl.multiple_of(rs * GRID_W, GRID_W)
        vt = na_ref[0, pl.ds(kv0, NA_KEYS), 2 * NA_D + t * LANES:2 * NA_D + (t + 1) * LANES]
        s = s_ref[u % 2, t]
        m = jnp.max(s, axis=-1, keepdims=True)
        p = jnp.exp(s - m)
        l = jnp.sum(p, axis=-1, keepdims=True)
        o2 = _dot(p.astype(BF16), vt) / l
        o_ref[0, u * GRID_W:(u + 1) * GRID_W, t * LANES:(t + 1) * LANES] = (
            jnp.where(first, o2[:GRID_W], o2[GRID_W:]))

    for t in range(n_pairs):
        scores(0, t)
    for u in range(rb):
        if u + 1 < rb:
            for t in range(n_pairs):
                scores(u + 1, t)
        for t in range(n_pairs):
            finish(u, t)


def _na_call(na, bias):
    B, S, w = na.shape
    rows = S // GRID_W
    rb = NA_RB
    kern = functools.partial(_na_kernel, rows=rows, rb=rb)
    return pl.pallas_call(
        kern,
        grid=(B, rows // rb),
        in_specs=[
            pl.BlockSpec((1, S, w), lambda b, i: (b, 0, 0)),
            _const_spec(bias.shape),
        ],
        out_specs=pl.BlockSpec((1, rb * GRID_W, NA_D), lambda b, i: (b, i, 0)),
        out_shape=jax.ShapeDtypeStruct((B, S, NA_D), F32),
        scratch_shapes=[pltpu.VMEM((2, NA_HEADS // 2, 2 * GRID_W, NA_KEYS), F32)],
        compiler_params=_compiler_params(2),
        name="na",
    )(na, bias)


def _post_kernel(x_ref, oa_ref, ob_ref, oc_ref, ga_ref, gb_ref, gc_ref, wout_ref, gffn_ref,
                 wg_ref, wu_ref, wd_ref, gfin_ref, o_ref, *, final):
    merged = jnp.concatenate(
        [_rms(oa_ref[...], ga_ref[...]), _rms(ob_ref[...], gb_ref[...]),
         _rms(oc_ref[...], gc_ref[...])], axis=-1).astype(BF16)
    x1 = x_ref[...] + _dot(merged, wout_ref[...])
    hb = _rms(x1, gffn_ref[...]).astype(BF16)
    g = _dot(hb, wg_ref[...])
    u = _dot(hb, wu_ref[...])
    act = (g * jax.nn.sigmoid(g) * u).astype(BF16)
    x2 = x1 + _dot(act, wd_ref[...])
    if final:
        x2 = _rms(x2, gfin_ref[...])
    o_ref[...] = x2


def _post_call(x, oa, ob, oc, lw, g_final, final):
    B, S, _ = x.shape
    T = B * S
    tm = POST_TM
    flat = lambda a: a.reshape(T, a.shape[-1])
    tile = lambda w: pl.BlockSpec((tm, w), lambda i: (i, 0))
    weights_a = [lw["g_a"], lw["g_b"], lw["g_c"], lw["w_out"], lw["g_ffn"],
                 lw["w_gate"], lw["w_up"], lw["w_down"], g_final]
    out = pl.pallas_call(
        functools.partial(_post_kernel, final=final),
        grid=(T // tm,),
        in_specs=[tile(D_MODEL), tile(NA_D), tile(MLA_HEADS * MLA_V), tile(GQA_HEADS * HEAD_DIM)]
        + [_const_spec(w.shape) for w in weights_a],
        out_specs=tile(D_MODEL),
        out_shape=jax.ShapeDtypeStruct((T, D_MODEL), F32),
        compiler_params=_compiler_params(1),
        name="post",
    )(flat(x), flat(oa), flat(ob), flat(oc), *weights_a)
    return out.reshape(B, S, D_MODEL)


def _rot32(w):
    return jnp.concatenate([-w[:, 16:32], w[:, 0:16]], axis=1)


def _rot64(w):
    return jnp.concatenate([_rot32(w[:, :32]), _rot32(w[:, 32:])], axis=1)


def _perm64(g):
    p = lambda v: jnp.concatenate([v[16:32], v[0:16]])
    return jnp.concatenate([p(g[:32]), p(g[32:])])


def _layer_weights(l, attn_norm, w_in, q_a_norm, kv_a_norm, w_uq, w_ukv, q_norm_c, k_norm_c,
                   g_out_a, g_out_b, g_out_c, w_out, ffn_norm, w_gate, w_up, w_down):
    wi = w_in[l]
    K = wi.shape[0]
    sizes = (NA_D, NA_D, NA_D, MLA_Q_RANK, MLA_KV_RANK, MLA_ROPE,
             GQA_HEADS * HEAD_DIM, GQA_KV_HEADS * HEAD_DIM, GQA_KV_HEADS * HEAD_DIM)
    offs = np.concatenate([[0], np.cumsum(sizes)])
    (wqa, wka, wva, wcq, wckv, wkpe, wqc, wkc, wvc) = [wi[:, offs[i]:offs[i + 1]] for i in range(9)]
    z = lambda n: jnp.zeros((K, n), F32)

    w_na = jnp.concatenate([wqa * NA_QSCALE, wka, wva], axis=1)
    kpe_tile = jnp.concatenate([z(MLA_NOPE), wkpe, z(LANES - MLA_QK)], axis=1)
    kpe_rot = jnp.concatenate([z(MLA_NOPE), _rot32(wkpe), z(LANES - MLA_QK)], axis=1)
    w_mla = jnp.concatenate([wcq, wckv, kpe_tile, kpe_rot], axis=1)

    qc_tiles, qc_rots = [], []
    for h in range(GQA_HEADS):
        wh = wqc[:, h * HEAD_DIM:(h + 1) * HEAD_DIM]
        pad = (lambda a: jnp.concatenate([a, z(HEAD_DIM)], axis=1)) if h // GQA_GROUP == 0 else \
              (lambda a: jnp.concatenate([z(HEAD_DIM), a], axis=1))
        qc_tiles.append(pad(wh))
        qc_rots.append(pad(_rot64(wh)))
    w_qc = jnp.concatenate(qc_tiles + qc_rots, axis=1)
    w_kc = jnp.concatenate(
        [wkc] + [_rot64(wkc[:, h * HEAD_DIM:(h + 1) * HEAD_DIM]) for h in range(GQA_KV_HEADS)], axis=1)

    uq = w_uq[l]
    zq = lambda n: jnp.zeros((MLA_Q_RANK, n), F32)
    uq_tiles, uq_rots = [], []
    for h in range(MLA_HEADS):
        wh = uq[:, h * MLA_QK:(h + 1) * MLA_QK]
        uq_tiles.append(jnp.concatenate([wh, zq(LANES - MLA_QK)], axis=1))
        uq_rots.append(jnp.concatenate([zq(MLA_NOPE), _rot32(wh[:, MLA_NOPE:]), zq(LANES - MLA_QK)], axis=1))
    w_uq_ext = jnp.concatenate(uq_tiles + uq_rots, axis=1)

    ukv = w_ukv[l]
    zk = jnp.zeros((MLA_KV_RANK, LANES - MLA_NOPE), F32)
    per = MLA_NOPE + MLA_V
    w_uk = jnp.concatenate(
        [jnp.concatenate([ukv[:, h * per:h * per + MLA_NOPE], zk], axis=1) for h in range(MLA_HEADS)], axis=1)
    w_uvt = jnp.concatenate([ukv[:, h * per + MLA_NOPE:(h + 1) * per] for h in range(MLA_HEADS)], axis=1).T

    two = lambda g: jnp.concatenate([g, g])
    row = lambda g: g.reshape(1, -1).astype(F32)
    return {
        "g_attn": row(attn_norm[l]),
        "w_na": w_na.astype(BF16), "w_mla": w_mla.astype(BF16), "w_qc": w_qc.astype(BF16),
        "w_kc": w_kc.astype(BF16), "w_vct": wvc.T.astype(BF16),
        "g_qa": row(q_a_norm[l]), "g_kva": row(kv_a_norm[l]),
        "w_uq": w_uq_ext.astype(BF16), "w_uk": w_uk.astype(BF16), "w_uvt": w_uvt.astype(BF16),
        "g_q": jnp.stack([two(q_norm_c[l]), two(_perm64(q_norm_c[l]))]).astype(F32),
        "g_k": jnp.stack([two(k_norm_c[l]), two(_perm64(k_norm_c[l]))]).astype(F32),
        "g_a": row(g_out_a[l]), "g_b": row(g_out_b[l]), "g_c": row(g_out_c[l]),
        "w_out": w_out[l].astype(BF16), "g_ffn": row(ffn_norm[l]),
        "w_gate": w_gate[l].astype(BF16), "w_up": w_up[l].astype(BF16), "w_down": w_down[l].astype(BF16),
    }


def _rope_tables(S):
    tok = jnp.arange(S)
    t = tok.astype(F32)
    row = (tok // GRID_W).astype(F32)
    col = (tok % GRID_W).astype(F32)
    half = MLA_ROPE // 2
    inv = ROPE_BASE ** (-(jnp.arange(half, dtype=F32) * 2.0) / MLA_ROPE)

    def cs(pos):
        ang = pos[:, None] * inv[None, :]
        c, s = jnp.cos(ang), jnp.sin(ang)
        return jnp.concatenate([c, c], axis=1), jnp.concatenate([s, s], axis=1)

    ct, st = cs(t)
    ones = jnp.ones((S, MLA_NOPE), F32)
    zeros = jnp.zeros((S, MLA_NOPE), F32)
    pad1 = jnp.ones((S, LANES - MLA_QK), F32)
    pad0 = jnp.zeros((S, LANES - MLA_QK), F32)
    cr, sr = cs(row)
    cc, sc = cs(col)
    ca = jnp.concatenate([cr, cc], axis=1)
    sa = jnp.concatenate([sr, sc], axis=1)
    return {
        "cos_m": jnp.concatenate([ones, ct, pad1], axis=1),
        "sin_m": jnp.concatenate([zeros, st, pad0], axis=1),
        "cos_a": jnp.concatenate([ca, ca], axis=1),
        "sin_a": jnp.concatenate([sa, sa], axis=1),
    }


def _na_bias_table(rpb_l):
    c = np.arange(GRID_W)
    cstart = np.clip(c - NA_COLS // 2, 0, GRID_W - NA_COLS)
    cp = np.arange(GRID_W)
    dc = cp[None, :] - c[:, None] + (NA_COLS - 1)
    valid = (cp[None, :] >= cstart[:, None]) & (cp[None, :] < cstart[:, None] + NA_COLS)
    onehot = ((dc[:, :, None] == np.arange(2 * NA_COLS - 1)) & valid[:, :, None]).astype(np.float32)
    t = jnp.einsum("hrd,cpd->hrcp", rpb_l.astype(F32), onehot, precision=lax.Precision.HIGHEST)
    t = jnp.where(valid[None, None], t, MASK_VALUE)
    b = jnp.stack([t[:, NA_ROWS - 1 - d:2 * NA_ROWS - 1 - d] for d in range(NA_ROWS)], axis=1)
    b = b.transpose(0, 1, 3, 2, 4).reshape(NA_HEADS // 2, 2, NA_ROWS, GRID_W, NA_KEYS)
    return b.transpose(0, 2, 1, 3, 4).reshape(NA_HEADS // 2, NA_ROWS, 2 * GRID_W, NA_KEYS)


MLA_K_TILES = tuple(range(MLA_HEADS))
MLA_V_ROWS = tuple(range(MLA_HEADS))
GQA_K_TILES = (0,) * GQA_HEADS
GQA_V_ROWS = tuple(h // GQA_GROUP for h in range(GQA_HEADS))


def _trunk(x, layers, biases, tabs, g_final):
    n_layers = len(layers)
    for l, lw in enumerate(layers):
        na, mq, mk, mvt, gq, gk, gvt = _pre_call(x, lw, tabs)
        oa = _na_call(na, biases[l])
        ob = _attn_call(mq, mk, mvt, k_tiles=MLA_K_TILES, v_rows=MLA_V_ROWS, name="mla")
        oc = _attn_call(gq, gk, gvt, k_tiles=GQA_K_TILES, v_rows=GQA_V_ROWS, name="gqa")
        x = _post_call(x, oa, ob, oc, lw, g_final, final=(l == n_layers - 1))
    return x


def kernel(x_prompt, x_sample, attn_norm, w_in, q_a_norm, kv_a_norm, w_uq, w_ukv, q_norm_c, k_norm_c, rpb, g_out_a, g_out_b, g_out_c, w_out, ffn_norm, w_gate, w_up, w_down, final_norm):
    depth = w_in.shape[0]
    layers = [_layer_weights(l, attn_norm, w_in, q_a_norm, kv_a_norm, w_uq, w_ukv, q_norm_c,
                             k_norm_c, g_out_a, g_out_b, g_out_c, w_out, ffn_norm, w_gate,
                             w_up, w_down) for l in range(depth)]
    biases = [_na_bias_table(rpb[l]) for l in range(depth)]
    g_final = final_norm.reshape(1, -1).astype(F32)
    outs = []
    for x in (x_prompt, x_sample):
        tabs = _rope_tables(x.shape[1])
        outs.append(_trunk(x, layers, biases, tabs, g_final))
    return tuple(outs)
```

```python
import functools
import math

import jax
import jax.numpy as jnp
import numpy as np
from jax import lax
from jax.experimental import pallas as pl
from jax.experimental.pallas import tpu as pltpu

D_MODEL = 1024
GRID_W = 64
HEAD_DIM = 64
NA_HEADS = 4
NA_ROWS = 8
NA_COLS = 16
NA_D = NA_HEADS * HEAD_DIM
MLA_HEADS = 6
MLA_Q_RANK = 384
MLA_KV_RANK = 256
MLA_NOPE = 64
MLA_ROPE = 32
MLA_V = 64
MLA_QK = MLA_NOPE + MLA_ROPE
GQA_HEADS = 6
GQA_KV_HEADS = 2
GQA_GROUP = GQA_HEADS // GQA_KV_HEADS
D_FF = 2816
ROPE_BASE = 10000.0
EPS = 1e-6

LANES = 128
SUBLANES = 8
VMEM_LIMIT_BYTES = 56 * 1024 * 1024
ATTN_VMEM_BUDGET_BYTES = 48 * 1024 * 1024

PRE_TM = 512
ATTN_TQ = 512
ATTN_TK = PRE_TM
ATTN_TILES = 4
ATTN_UNROLL = 4
ONES_ROWS = 16
V_ROWS = HEAD_DIM + ONES_ROWS
NA_RB = 16
NA_KEYS = NA_ROWS * GRID_W
POST_TM = 512

LOG2E = 1.4426950408889634
MLA_QSCALE = LOG2E / math.sqrt(MLA_QK)
GQA_QSCALE = LOG2E / math.sqrt(HEAD_DIM)
NA_QSCALE = 1.0 / math.sqrt(HEAD_DIM)
MASK_VALUE = -1e30

BF16 = jnp.bfloat16
F32 = jnp.float32
NT_DIMS = (((1,), (1,)), ((), ()))


def _dot(a, b):
    return jnp.dot(a, b, preferred_element_type=F32)


def _dot_nt(a, b):
    return lax.dot_general(a, b, NT_DIMS, preferred_element_type=F32)


def _rms(x, g):
    return x * lax.rsqrt(jnp.mean(x * x, axis=-1, keepdims=True) + EPS) * g


def _with_ones_rows(vt, n_heads):
    ones = jnp.ones((ONES_ROWS, vt.shape[1]), F32)
    parts = []
    for h in range(n_heads):
        parts += [vt[h * HEAD_DIM:(h + 1) * HEAD_DIM], ones]
    return jnp.concatenate(parts, axis=0)


def _compiler_params(n_axes):
    return pltpu.CompilerParams(
        dimension_semantics=("arbitrary",) * n_axes,
        vmem_limit_bytes=VMEM_LIMIT_BYTES,
    )


def _const_spec(shape):
    nd = len(shape)
    return pl.BlockSpec(shape, lambda *_: (0,) * nd, pipeline_mode=pl.Buffered(1))


def _pre_kernel(x_ref, gattn_ref, wna_ref, wmla_ref, wqc_ref, wkc_ref, wvct_ref,
                qan_ref, kvan_ref, wuq_ref, wuk_ref, wuvt_ref, gq_ref, gk_ref,
                cosm_ref, sinm_ref, cosa_ref, sina_ref,
                na_ref, mq_ref, mk_ref, mvt_ref, gqo_ref, gko_ref, gvt_ref):
    x = x_ref[0]
    xb = _rms(x, gattn_ref[...]).astype(BF16)

    na_ref[0] = _dot(xb, wna_ref[...]).astype(BF16)

    cosm = cosm_ref[...]
    sinm = sinm_ref[...]
    cosa = cosa_ref[...]
    sina = sina_ref[...]

    lat = _dot(xb, wmla_ref[...])
    cq = lat[:, :MLA_Q_RANK]
    ckv = lat[:, MLA_Q_RANK:MLA_Q_RANK + MLA_KV_RANK]
    o = MLA_Q_RANK + MLA_KV_RANK
    kpe = lat[:, o:o + LANES] * cosm + lat[:, o + LANES:o + 2 * LANES] * sinm

    cqn = _rms(cq, qan_ref[...]).astype(BF16)
    q2 = _dot(cqn, wuq_ref[...])
    hw = MLA_HEADS * LANES
    for h in range(MLA_HEADS):
        sl = slice(h * LANES, (h + 1) * LANES)
        qh = q2[:, sl] * cosm + q2[:, hw + h * LANES:hw + (h + 1) * LANES] * sinm
        mq_ref[0, :, sl] = (qh * MLA_QSCALE).astype(BF16)

    ckvn = _rms(ckv, kvan_ref[...]).astype(BF16)
    kn = _dot(ckvn, wuk_ref[...])
    for h in range(MLA_HEADS):
        sl = slice(h * LANES, (h + 1) * LANES)
        mk_ref[0, :, sl] = (kn[:, sl] + kpe).astype(BF16)
    mvt_ref[0, 0] = _with_ones_rows(_dot_nt(wuvt_ref[...], ckvn), MLA_HEADS).astype(BF16)

    gq = gq_ref[...]
    gqc = cosa * gq[0:1]
    gqs = sina * gq[1:2]
    qc2 = _dot(xb, wqc_ref[...])
    hw = GQA_HEADS * LANES
    for h in range(GQA_HEADS):
        sl = slice(h * LANES, (h + 1) * LANES)
        xh = qc2[:, sl]
        xr = qc2[:, hw + h * LANES:hw + (h + 1) * LANES]
        ms = jnp.sum(xh * xh, axis=-1, keepdims=True) * (1.0 / HEAD_DIM)
        sc = lax.rsqrt(ms + EPS) * GQA_QSCALE
        gqo_ref[0, :, sl] = ((xh * gqc + xr * gqs) * sc).astype(BF16)

    gk = gk_ref[...]
    kc2 = _dot(xb, wkc_ref[...])
    xk = kc2[:, :LANES]
    xkr = kc2[:, LANES:]
    lane = lax.broadcasted_iota(jnp.int32, xk.shape, 1)
    first = lane < HEAD_DIM
    sq = xk * xk
    ms0 = jnp.sum(jnp.where(first, sq, 0.0), axis=-1, keepdims=True) * (1.0 / HEAD_DIM)
    ms1 = jnp.sum(jnp.where(first, 0.0, sq), axis=-1, keepdims=True) * (1.0 / HEAD_DIM)
    sc = jnp.where(first, lax.rsqrt(ms0 + EPS), lax.rsqrt(ms1 + EPS))
    gko_ref[0] = ((xk * (cosa * gk[0:1]) + xkr * (sina * gk[1:2])) * sc).astype(BF16)
    gvt_ref[0, 0] = _with_ones_rows(_dot_nt(wvct_ref[...], xb), GQA_KV_HEADS).astype(BF16)


def _pre_call(x, lw, tabs):
    B, S, _ = x.shape
    tm = PRE_TM
    nt = S // tm
    tile = lambda w: pl.BlockSpec((1, tm, w), lambda b, i: (b, i, 0))
    tab = pl.BlockSpec((tm, LANES), lambda b, i: (i, 0))
    weights = [lw["g_attn"], lw["w_na"], lw["w_mla"], lw["w_qc"], lw["w_kc"], lw["w_vct"],
               lw["g_qa"], lw["g_kva"], lw["w_uq"], lw["w_uk"], lw["w_uvt"], lw["g_q"], lw["g_k"]]
    out_shape = (
        jax.ShapeDtypeStruct((B, S, 3 * NA_D), BF16),
        jax.ShapeDtypeStruct((B, S, MLA_HEADS * LANES), BF16),
        jax.ShapeDtypeStruct((B, S, MLA_HEADS * LANES), BF16),
        jax.ShapeDtypeStruct((B, nt, MLA_HEADS * V_ROWS, tm), BF16),
        jax.ShapeDtypeStruct((B, S, GQA_HEADS * LANES), BF16),
        jax.ShapeDtypeStruct((B, S, GQA_KV_HEADS * HEAD_DIM), BF16),
        jax.ShapeDtypeStruct((B, nt, GQA_KV_HEADS * V_ROWS, tm), BF16),
    )
    out_specs = (
        tile(3 * NA_D), tile(MLA_HEADS * LANES), tile(MLA_HEADS * LANES),
        pl.BlockSpec((1, 1, MLA_HEADS * V_ROWS, tm), lambda b, i: (b, i, 0, 0)),
        tile(GQA_HEADS * LANES), tile(GQA_KV_HEADS * HEAD_DIM),
        pl.BlockSpec((1, 1, GQA_KV_HEADS * V_ROWS, tm), lambda b, i: (b, i, 0, 0)),
    )
    return pl.pallas_call(
        _pre_kernel,
        grid=(B, nt),
        in_specs=[tile(D_MODEL)] + [_const_spec(w.shape) for w in weights] + [tab] * 4,
        out_specs=out_specs,
        out_shape=out_shape,
        compiler_params=_compiler_params(2),
        name="pre",
    )(x, *weights, tabs["cos_m"], tabs["sin_m"], tabs["cos_a"], tabs["sin_a"])


def _column_max(x):
    rows = x.shape[0]
    while rows > SUBLANES:
        rows //= 2
        x = jnp.maximum(x[:rows], x[rows:])
    return jnp.max(x, axis=0, keepdims=True)


def _attn_kernel(q_ref, k_ref, vt_ref, o_ref, m_ref, acc_ref, s_ref, cm_ref, qt_ref, *, n_heads,
                 k_tiles, v_rows, n_chunks, n_tiles, tq, tk):
    def reset():
        m_ref[...] = jnp.full(m_ref.shape, -jnp.inf, F32)
        acc_ref[...] = jnp.zeros(acc_ref.shape, F32)

    def scores(t, j, slot, h):
        start = pl.multiple_of(j * tk, tk)
        kt = k_tiles[h]
        qht = qt_ref[t, h * LANES:(h + 1) * LANES, :]
        kc = k_ref[0, pl.ds(start, tk), kt * LANES:(kt + 1) * LANES]
        st = _dot(kc, qht)
        s_ref[slot, h] = st
        cm_ref[slot, h] = _column_max(st)

    def accumulate(j, slot, h):
        vr = v_rows[h]
        m_old = m_ref[h]
        m_new = jnp.maximum(m_old, cm_ref[slot, h])
        alpha = jnp.exp2(m_old - m_new)
        p = jnp.exp2(s_ref[slot, h] - m_new).astype(BF16)
        vc = vt_ref[0, j, vr * V_ROWS:(vr + 1) * V_ROWS, :]
        acc_ref[h] = alpha * acc_ref[h] + _dot(vc, p)
        m_ref[h] = m_new

    def step(t, j, slot, t_next, j_next):
        for h in range(n_heads):
            scores(t_next, j_next, 1 - slot, h)
            accumulate(j, slot, h)

    def tile_body(t, carry):
        def chunks_body(jb, c):
            j0 = jb * ATTN_UNROLL
            for u in range(ATTN_UNROLL - 1):
                step(t, j0 + u, u % 2, t, j0 + u + 1)
            j_last = j0 + ATTN_UNROLL - 1
            wrap = (j_last + 1 == n_chunks).astype(jnp.int32)
            step(t, j_last, (ATTN_UNROLL - 1) % 2,
                 jnp.minimum(t + wrap, n_tiles - 1), (1 - wrap) * (j_last + 1))
            return c

        lax.fori_loop(0, n_chunks // ATTN_UNROLL, chunks_body, 0)
        outs = []
        for h in range(n_heads):
            acc = acc_ref[h]
            outs.append(acc[:HEAD_DIM] / acc[HEAD_DIM:HEAD_DIM + 1])
        o_ref[0, pl.ds(pl.multiple_of(t * tq, tq), tq), :] = jnp.concatenate(outs, axis=0).T
        reset()
        return carry

    for t in range(n_tiles):
        qt_ref[t] = q_ref[0, t * tq:(t + 1) * tq, :].astype(F32).T.astype(BF16)
    reset()
    for h in range(n_heads):
        scores(0, 0, 0, h)
    lax.fori_loop(0, n_tiles, tile_body, 0)


def _attn_call(q, k, vt, *, k_tiles, v_rows, name):
    B, S, qw = q.shape
    n_heads = qw // LANES
    tq, tk, n_tiles = ATTN_TQ, ATTN_TK, ATTN_TILES
    n_chunks = S // tk
    rows = tq * n_tiles
    assert S % rows == 0 and S % tk == 0 and n_chunks % ATTN_UNROLL == 0, (S, rows, tk)
    assert ATTN_UNROLL % 2 == 0
    kw = k.shape[2]
    vw = vt.shape[2]
    ow = n_heads * HEAD_DIM
    kv_bytes = S * (kw + vw) * 2
    other_bytes = 2 * rows * (qw * 2 + ow * 4) + 2 * n_heads * tk * tq * 4
    kv_buffers = 2 if 2 * kv_bytes + other_bytes < ATTN_VMEM_BUDGET_BYTES else 1
    kern = functools.partial(_attn_kernel, n_heads=n_heads, k_tiles=k_tiles, v_rows=v_rows,
                             n_chunks=n_chunks, n_tiles=n_tiles, tq=tq, tk=tk)
    return pl.pallas_call(
        kern,
        grid=(B, S // rows),
        in_specs=[
            pl.BlockSpec((1, rows, qw), lambda b, i: (b, i, 0)),
            pl.BlockSpec((1, S, kw), lambda b, i: (b, 0, 0), pipeline_mode=pl.Buffered(kv_buffers)),
            pl.BlockSpec((1, n_chunks, vw, tk), lambda b, i: (b, 0, 0, 0),
                         pipeline_mode=pl.Buffered(kv_buffers)),
        ],
        out_specs=pl.BlockSpec((1, rows, ow), lambda b, i: (b, i, 0)),
        out_shape=jax.ShapeDtypeStruct((B, S, ow), F32),
        scratch_shapes=[pltpu.VMEM((n_heads, 1, tq), F32),
                        pltpu.VMEM((n_heads, V_ROWS, tq), F32),
                        pltpu.VMEM((2, n_heads, tk, tq), F32),
                        pltpu.VMEM((2, n_heads, 1, tq), F32),
                        pltpu.VMEM((n_tiles, qw, tq), BF16)],
        compiler_params=_compiler_params(2),
        name=name,
    )(q, k, vt)


def _na_kernel(na_ref, bias_ref, o_ref, s_ref, *, rows, rb):
    i = pl.program_id(1)
    lane = lax.broadcasted_iota(jnp.int32, (GRID_W, LANES), 1)
    first = lane < HEAD_DIM
    n_pairs = NA_HEADS // 2

    def window_start(u):
        r = i * rb + u
        rs = jnp.clip(r - NA_ROWS // 2, 0, rows - NA_ROWS)
        return r, rs

    def scores(u, t):
        r, rs = window_start(u)
        kv0 = pl.multiple_of(rs * GRID_W, GRID_W)
        qt = na_ref[0, pl.ds(pl.multiple_of(r * GRID_W, GRID_W), GRID_W), t * LANES:(t + 1) * LANES]
        kt = na_ref[0, pl.ds(kv0, NA_KEYS), NA_D + t * LANES:NA_D + (t + 1) * LANES]
        zero = jnp.zeros_like(qt)
        qm = jnp.concatenate([jnp.where(first, qt, zero), jnp.where(first, zero, qt)], axis=0)
        s_ref[u % 2, t] = _dot_nt(qm, kt) + bias_ref[t, r - rs]

    def finish(u, t):
        _, rs = window_start(u)
        kv0 = pl.multiple_of(rs * GRID_W, GRID_W)
        vt = na_ref[0, pl.ds(kv0, NA_KEYS), 2 * NA_D + t * LANES:2 * NA_D + (t + 1) * LANES]
        s = s_ref[u % 2, t]
        m = jnp.max(s, axis=-1, keepdims=True)
        p = jnp.exp(s - m)
        l = jnp.sum(p, axis=-1, keepdims=True)
        o2 = _dot(p.astype(BF16), vt) / l
        o_ref[0, u * GRID_W:(u + 1) * GRID_W, t * LANES:(t + 1) * LANES] = (
            jnp.where(first, o2[:GRID_W], o2[GRID_W:]))

    for t in range(n_pairs):
        scores(0, t)
    for u in range(rb):
        if u + 1 < rb:
            for t in range(n_pairs):
                scores(u + 1, t)
        for t in range(n_pairs):
            finish(u, t)


def _na_call(na, bias):
    B, S, w = na.shape
    rows = S // GRID_W
    rb = NA_RB
    kern = functools.partial(_na_kernel, rows=rows, rb=rb)
    return pl.pallas_call(
        kern,
        grid=(B, rows // rb),
        in_specs=[
            pl.BlockSpec((1, S, w), lambda b, i: (b, 0, 0)),
            _const_spec(bias.shape),
        ],
        out_specs=pl.BlockSpec((1, rb * GRID_W, NA_D), lambda b, i: (b, i, 0)),
        out_shape=jax.ShapeDtypeStruct((B, S, NA_D), F32),
        scratch_shapes=[pltpu.VMEM((2, NA_HEADS // 2, 2 * GRID_W, NA_KEYS), F32)],
        compiler_params=_compiler_params(2),
        name="na",
    )(na, bias)


def _post_kernel(x_ref, oa_ref, ob_ref, oc_ref, ga_ref, gb_ref, gc_ref, wout_ref, gffn_ref,
                 wg_ref, wu_ref, wd_ref, gfin_ref, o_ref, *, final):
    merged = jnp.concatenate(
        [_rms(oa_ref[...], ga_ref[...]), _rms(ob_ref[...], gb_ref[...]),
         _rms(oc_ref[...], gc_ref[...])], axis=-1).astype(BF16)
    x1 = x_ref[...] + _dot(merged, wout_ref[...])
    hb = _rms(x1, gffn_ref[...]).astype(BF16)
    g = _dot(hb, wg_ref[...])
    u = _dot(hb, wu_ref[...])
    act = (g * jax.nn.sigmoid(g) * u).astype(BF16)
    x2 = x1 + _dot(act, wd_ref[...])
    if final:
        x2 = _rms(x2, gfin_ref[...])
    o_ref[...] = x2


def _post_call(x, oa, ob, oc, lw, g_final, final):
    B, S, _ = x.shape
    T = B * S
    tm = POST_TM
    flat = lambda a: a.reshape(T, a.shape[-1])
    tile = lambda w: pl.BlockSpec((tm, w), lambda i: (i, 0))
    weights_a = [lw["g_a"], lw["g_b"], lw["g_c"], lw["w_out"], lw["g_ffn"],
                 lw["w_gate"], lw["w_up"], lw["w_down"], g_final]
    out = pl.pallas_call(
        functools.partial(_post_kernel, final=final),
        grid=(T // tm,),
        in_specs=[tile(D_MODEL), tile(NA_D), tile(MLA_HEADS * MLA_V), tile(GQA_HEADS * HEAD_DIM)]
        + [_const_spec(w.shape) for w in weights_a],
        out_specs=tile(D_MODEL),
        out_shape=jax.ShapeDtypeStruct((T, D_MODEL), F32),
        compiler_params=_compiler_params(1),
        name="post",
    )(flat(x), flat(oa), flat(ob), flat(oc), *weights_a)
    return out.reshape(B, S, D_MODEL)


def _rot32(w):
    return jnp.concatenate([-w[:, 16:32], w[:, 0:16]], axis=1)


def _rot64(w):
    return jnp.concatenate([_rot32(w[:, :32]), _rot32(w[:, 32:])], axis=1)


def _perm64(g):
    p = lambda v: jnp.concatenate([v[16:32], v[0:16]])
    return jnp.concatenate([p(g[:32]), p(g[32:])])


def _layer_weights(l, attn_norm, w_in, q_a_norm, kv_a_norm, w_uq, w_ukv, q_norm_c, k_norm_c,
                   g_out_a, g_out_b, g_out_c, w_out, ffn_norm, w_gate, w_up, w_down):
    wi = w_in[l]
    K = wi.shape[0]
    sizes = (NA_D, NA_D, NA_D, MLA_Q_RANK, MLA_KV_RANK, MLA_ROPE,
             GQA_HEADS * HEAD_DIM, GQA_KV_HEADS * HEAD_DIM, GQA_KV_HEADS * HEAD_DIM)
    offs = np.concatenate([[0], np.cumsum(sizes)])
    (wqa, wka, wva, wcq, wckv, wkpe, wqc, wkc, wvc) = [wi[:, offs[i]:offs[i + 1]] for i in range(9)]
    z = lambda n: jnp.zeros((K, n), F32)

    w_na = jnp.concatenate([wqa * NA_QSCALE, wka, wva], axis=1)
    kpe_tile = jnp.concatenate([z(MLA_NOPE), wkpe, z(LANES - MLA_QK)], axis=1)
    kpe_rot = jnp.concatenate([z(MLA_NOPE), _rot32(wkpe), z(LANES - MLA_QK)], axis=1)
    w_mla = jnp.concatenate([wcq, wckv, kpe_tile, kpe_rot], axis=1)

    qc_tiles, qc_rots = [], []
    for h in range(GQA_HEADS):
        wh = wqc[:, h * HEAD_DIM:(h + 1) * HEAD_DIM]
        pad = (lambda a: jnp.concatenate([a, z(HEAD_DIM)], axis=1)) if h // GQA_GROUP == 0 else \
              (lambda a: jnp.concatenate([z(HEAD_DIM), a], axis=1))
        qc_tiles.append(pad(wh))
        qc_rots.append(pad(_rot64(wh)))
    w_qc = jnp.concatenate(qc_tiles + qc_rots, axis=1)
    w_kc = jnp.concatenate(
        [wkc] + [_rot64(wkc[:, h * HEAD_DIM:(h + 1) * HEAD_DIM]) for h in range(GQA_KV_HEADS)], axis=1)

    uq = w_uq[l]
    zq = lambda n: jnp.zeros((MLA_Q_RANK, n), F32)
    uq_tiles, uq_rots = [], []
    for h in range(MLA_HEADS):
        wh = uq[:, h * MLA_QK:(h + 1) * MLA_QK]
        uq_tiles.append(jnp.concatenate([wh, zq(LANES - MLA_QK)], axis=1))
        uq_rots.append(jnp.concatenate([zq(MLA_NOPE), _rot32(wh[:, MLA_NOPE:]), zq(LANES - MLA_QK)], axis=1))
    w_uq_ext = jnp.concatenate(uq_tiles + uq_rots, axis=1)

    ukv = w_ukv[l]
    zk = jnp.zeros((MLA_KV_RANK, LANES - MLA_NOPE), F32)
    per = MLA_NOPE + MLA_V
    w_uk = jnp.concatenate(
        [jnp.concatenate([ukv[:, h * per:h * per + MLA_NOPE], zk], axis=1) for h in range(MLA_HEADS)], axis=1)
    w_uvt = jnp.concatenate([ukv[:, h * per + MLA_NOPE:(h + 1) * per] for h in range(MLA_HEADS)], axis=1).T

    two = lambda g: jnp.concatenate([g, g])
    row = lambda g: g.reshape(1, -1).astype(F32)
    return {
        "g_attn": row(attn_norm[l]),
        "w_na": w_na.astype(BF16), "w_mla": w_mla.astype(BF16), "w_qc": w_qc.astype(BF16),
        "w_kc": w_kc.astype(BF16), "w_vct": wvc.T.astype(BF16),
        "g_qa": row(q_a_norm[l]), "g_kva": row(kv_a_norm[l]),
        "w_uq": w_uq_ext.astype(BF16), "w_uk": w_uk.astype(BF16), "w_uvt": w_uvt.astype(BF16),
        "g_q": jnp.stack([two(q_norm_c[l]), two(_perm64(q_norm_c[l]))]).astype(F32),
        "g_k": jnp.stack([two(k_norm_c[l]), two(_perm64(k_norm_c[l]))]).astype(F32),
        "g_a": row(g_out_a[l]), "g_b": row(g_out_b[l]), "g_c": row(g_out_c[l]),
        "w_out": w_out[l].astype(BF16), "g_ffn": row(ffn_norm[l]),
        "w_gate": w_gate[l].astype(BF16), "w_up": w_up[l].astype(BF16), "w_down": w_down[l].astype(BF16),
    }


def _rope_tables(S):
    tok = jnp.arange(S)
    t = tok.astype(F32)
    row = (tok // GRID_W).astype(F32)
    col = (tok % GRID_W).astype(F32)
    half = MLA_ROPE // 2
    inv = ROPE_BASE ** (-(jnp.arange(half, dtype=F32) * 2.0) / MLA_ROPE)

    def cs(pos):
        ang = pos[:, None] * inv[None, :]
        c, s = jnp.cos(ang), jnp.sin(ang)
        return jnp.concatenate([c, c], axis=1), jnp.concatenate([s, s], axis=1)

    ct, st = cs(t)
    ones = jnp.ones((S, MLA_NOPE), F32)
    zeros = jnp.zeros((S, MLA_NOPE), F32)
    pad1 = jnp.ones((S, LANES - MLA_QK), F32)
    pad0 = jnp.zeros((S, LANES - MLA_QK), F32)
    cr, sr = cs(row)
    cc, sc = cs(col)
    ca = jnp.concatenate([cr, cc], axis=1)
    sa = jnp.concatenate([sr, sc], axis=1)
    return {
        "cos_m": jnp.concatenate([ones, ct, pad1], axis=1),
        "sin_m": jnp.concatenate([zeros, st, pad0], axis=1),
        "cos_a": jnp.concatenate([ca, ca], axis=1),
        "sin_a": jnp.concatenate([sa, sa], axis=1),
    }


def _na_bias_table(rpb_l):
    c = np.arange(GRID_W)
    cstart = np.clip(c - NA_COLS // 2, 0, GRID_W - NA_COLS)
    cp = np.arange(GRID_W)
    dc = cp[None, :] - c[:, None] + (NA_COLS - 1)
    valid = (cp[None, :] >= cstart[:, None]) & (cp[None, :] < cstart[:, None] + NA_COLS)
    onehot = ((dc[:, :, None] == np.arange(2 * NA_COLS - 1)) & valid[:, :, None]).astype(np.float32)
    t = jnp.einsum("hrd,cpd->hrcp", rpb_l.astype(F32), onehot, precision=lax.Precision.HIGHEST)
    t = jnp.where(valid[None, None], t, MASK_VALUE)
    b = jnp.stack([t[:, NA_ROWS - 1 - d:2 * NA_ROWS - 1 - d] for d in range(NA_ROWS)], axis=1)
    b = b.transpose(0, 1, 3, 2, 4).reshape(NA_HEADS // 2, 2, NA_ROWS, GRID_W, NA_KEYS)
    return b.transpose(0, 2, 1, 3, 4).reshape(NA_HEADS // 2, NA_ROWS, 2 * GRID_W, NA_KEYS)


MLA_K_TILES = tuple(range(MLA_HEADS))
MLA_V_ROWS = tuple(range(MLA_HEADS))
GQA_K_TILES = (0,) * GQA_HEADS
GQA_V_ROWS = tuple(h // GQA_GROUP for h in range(GQA_HEADS))


def _trunk(x, layers, biases, tabs, g_final):
    n_layers = len(layers)
    for l, lw in enumerate(layers):
        na, mq, mk, mvt, gq, gk, gvt = _pre_call(x, lw, tabs)
        oa = _na_call(na, biases[l])
        ob = _attn_call(mq, mk, mvt, k_tiles=MLA_K_TILES, v_rows=MLA_V_ROWS, name="mla")
        oc = _attn_call(gq, gk, gvt, k_tiles=GQA_K_TILES, v_rows=GQA_V_ROWS, name="gqa")
        x = _post_call(x, oa, ob, oc, lw, g_final, final=(l == n_layers - 1))
    return x


def kernel(x_prompt, x_sample, attn_norm, w_in, q_a_norm, kv_a_norm, w_uq, w_ukv, q_norm_c, k_norm_c, rpb, g_out_a, g_out_b, g_out_c, w_out, ffn_norm, w_gate, w_up, w_down, final_norm):
    depth = w_in.shape[0]
    layers = [_layer_weights(l, attn_norm, w_in, q_a_norm, kv_a_norm, w_uq, w_ukv, q_norm_c,
                             k_norm_c, g_out_a, g_out_b, g_out_c, w_out, ffn_norm, w_gate,
                             w_up, w_down) for l in range(depth)]
    biases = [_na_bias_table(rpb[l]) for l in range(depth)]
    g_final = final_norm.reshape(1, -1).astype(F32)
    outs = []
    for x in (x_prompt, x_sample):
        tabs = _rope_tables(x.shape[1])
        outs.append(_trunk(x, layers, biases, tabs, g_final))
    return tuple(outs)
```

```python
import functools
import math

import jax
import jax.numpy as jnp
import numpy as np
from jax import lax
from jax.experimental import pallas as pl
from jax.experimental.pallas import tpu as pltpu

D_MODEL = 1024
GRID_W = 64
HEAD_DIM = 64
NA_HEADS = 4
NA_ROWS = 8
NA_COLS = 16
NA_D = NA_HEADS * HEAD_DIM
MLA_HEADS = 6
MLA_Q_RANK = 384
MLA_KV_RANK = 256
MLA_NOPE = 64
MLA_ROPE = 32
MLA_V = 64
MLA_QK = MLA_NOPE + MLA_ROPE
GQA_HEADS = 6
GQA_KV_HEADS = 2
GQA_GROUP = GQA_HEADS // GQA_KV_HEADS
D_FF = 2816
ROPE_BASE = 10000.0
EPS = 1e-6

LANES = 128
SUBLANES = 8
VMEM_LIMIT_BYTES = 56 * 1024 * 1024
ATTN_VMEM_BUDGET_BYTES = 48 * 1024 * 1024

PRE_TM = 512
ATTN_TQ = 512
ATTN_TK = PRE_TM
ATTN_TILES = 4
ATTN_UNROLL = 4
ONES_ROWS = 16
V_ROWS = HEAD_DIM + ONES_ROWS
NA_RB = 16
NA_KEYS = NA_ROWS * GRID_W
POST_TM = 512

LOG2E = 1.4426950408889634
MLA_QSCALE = LOG2E / math.sqrt(MLA_QK)
GQA_QSCALE = LOG2E / math.sqrt(HEAD_DIM)
NA_QSCALE = 1.0 / math.sqrt(HEAD_DIM)
MASK_VALUE = -1e30

BF16 = jnp.bfloat16
F32 = jnp.float32
NT_DIMS = (((1,), (1,)), ((), ()))


def _dot(a, b):
    return jnp.dot(a, b, preferred_element_type=F32)


def _dot_nt(a, b):
    return lax.dot_general(a, b, NT_DIMS, preferred_element_type=F32)


def _rms(x, g):
    return x * lax.rsqrt(jnp.mean(x * x, axis=-1, keepdims=True) + EPS) * g


def _with_ones_rows(vt, n_heads):
    ones = jnp.ones((ONES_ROWS, vt.shape[1]), F32)
    parts = []
    for h in range(n_heads):
        parts += [vt[h * HEAD_DIM:(h + 1) * HEAD_DIM], ones]
    return jnp.concatenate(parts, axis=0)


def _compiler_params(n_axes):
    return pltpu.CompilerParams(
        dimension_semantics=("arbitrary",) * n_axes,
        vmem_limit_bytes=VMEM_LIMIT_BYTES,
    )


def _const_spec(shape):
    nd = len(shape)
    return pl.BlockSpec(shape, lambda *_: (0,) * nd, pipeline_mode=pl.Buffered(1))


def _pre_kernel(x_ref, gattn_ref, wna_ref, wmla_ref, wqc_ref, wkc_ref, wvc_ref,
                qan_ref, kvan_ref, wuq_ref, wuk_ref, wuv_ref, gq_ref, gk_ref,
                cosm_ref, sinm_ref, cosa_ref, sina_ref,
                na_ref, mq_ref, mk_ref, mvt_ref, gqo_ref, gko_ref, gvt_ref):
    x = x_ref[0]
    xb = _rms(x, gattn_ref[...]).astype(BF16)

    na_ref[0] = _dot(xb, wna_ref[...]).astype(BF16)

    cosm = cosm_ref[...]
    sinm = sinm_ref[...]
    cosa = cosa_ref[...]
    sina = sina_ref[...]

    lat = _dot(xb, wmla_ref[...])
    cq = lat[:, :MLA_Q_RANK]
    ckv = lat[:, MLA_Q_RANK:MLA_Q_RANK + MLA_KV_RANK]
    o = MLA_Q_RANK + MLA_KV_RANK
    kpe = lat[:, o:o + LANES] * cosm + lat[:, o + LANES:o + 2 * LANES] * sinm

    cqn = _rms(cq, qan_ref[...]).astype(BF16)
    q2 = _dot(cqn, wuq_ref[...])
    hw = MLA_HEADS * LANES
    for h in range(MLA_HEADS):
        sl = slice(h * LANES, (h + 1) * LANES)
        qh = q2[:, sl] * cosm + q2[:, hw + h * LANES:hw + (h + 1) * LANES] * sinm
        mq_ref[0, :, sl] = (qh * MLA_QSCALE).astype(BF16)

    ckvn = _rms(ckv, kvan_ref[...]).astype(BF16)
    kn = _dot(ckvn, wuk_ref[...])
    for h in range(MLA_HEADS):
        sl = slice(h * LANES, (h + 1) * LANES)
        mk_ref[0, :, sl] = (kn[:, sl] + kpe).astype(BF16)
    mvt_ref[0, 0] = _with_ones_rows(_dot(ckvn, wuv_ref[...]).T, MLA_HEADS).astype(BF16)

    gq = gq_ref[...]
    gqc = cosa * gq[0:1]
    gqs = sina * gq[1:2]
    qc2 = _dot(xb, wqc_ref[...])
    hw = GQA_HEADS * LANES
    for h in range(GQA_HEADS):
        sl = slice(h * LANES, (h + 1) * LANES)
        xh = qc2[:, sl]
        xr = qc2[:, hw + h * LANES:hw + (h + 1) * LANES]
        ms = jnp.sum(xh * xh, axis=-1, keepdims=True) * (1.0 / HEAD_DIM)
        sc = lax.rsqrt(ms + EPS) * GQA_QSCALE
        gqo_ref[0, :, sl] = ((xh * gqc + xr * gqs) * sc).astype(BF16)

    gk = gk_ref[...]
    kc2 = _dot(xb, wkc_ref[...])
    xk = kc2[:, :LANES]
    xkr = kc2[:, LANES:]
    lane = lax.broadcasted_iota(jnp.int32, xk.shape, 1)
    first = lane < HEAD_DIM
    sq = xk * xk
    ms0 = jnp.sum(jnp.where(first, sq, 0.0), axis=-1, keepdims=True) * (1.0 / HEAD_DIM)
    ms1 = jnp.sum(jnp.where(first, 0.0, sq), axis=-1, keepdims=True) * (1.0 / HEAD_DIM)
    sc = jnp.where(first, lax.rsqrt(ms0 + EPS), lax.rsqrt(ms1 + EPS))
    gko_ref[0] = ((xk * (cosa * gk[0:1]) + xkr * (sina * gk[1:2])) * sc).astype(BF16)
    gvt_ref[0, 0] = _with_ones_rows(_dot(xb, wvc_ref[...]).T, GQA_KV_HEADS).astype(BF16)


def _pre_call(x, lw, tabs):
    B, S, _ = x.shape
    tm = PRE_TM
    nt = S // tm
    tile = lambda w: pl.BlockSpec((1, tm, w), lambda b, i: (b, i, 0))
    tab = pl.BlockSpec((tm, LANES), lambda b, i: (i, 0))
    weights = [lw["g_attn"], lw["w_na"], lw["w_mla"], lw["w_qc"], lw["w_kc"], lw["w_vc"],
               lw["g_qa"], lw["g_kva"], lw["w_uq"], lw["w_uk"], lw["w_uv"], lw["g_q"], lw["g_k"]]
    out_shape = (
        jax.ShapeDtypeStruct((B, S, 3 * NA_D), BF16),
        jax.ShapeDtypeStruct((B, S, MLA_HEADS * LANES), BF16),
        jax.ShapeDtypeStruct((B, S, MLA_HEADS * LANES), BF16),
        jax.ShapeDtypeStruct((B, nt, MLA_HEADS * V_ROWS, tm), BF16),
        jax.ShapeDtypeStruct((B, S, GQA_HEADS * LANES), BF16),
        jax.ShapeDtypeStruct((B, S, GQA_KV_HEADS * HEAD_DIM), BF16),
        jax.ShapeDtypeStruct((B, nt, GQA_KV_HEADS * V_ROWS, tm), BF16),
    )
    out_specs = (
        tile(3 * NA_D), tile(MLA_HEADS * LANES), tile(MLA_HEADS * LANES),
        pl.BlockSpec((1, 1, MLA_HEADS * V_ROWS, tm), lambda b, i: (b, i, 0, 0)),
        tile(GQA_HEADS * LANES), tile(GQA_KV_HEADS * HEAD_DIM),
        pl.BlockSpec((1, 1, GQA_KV_HEADS * V_ROWS, tm), lambda b, i: (b, i, 0, 0)),
    )
    return pl.pallas_call(
        _pre_kernel,
        grid=(B, nt),
        in_specs=[tile(D_MODEL)] + [_const_spec(w.shape) for w in weights] + [tab] * 4,
        out_specs=out_specs,
        out_shape=out_shape,
        compiler_params=_compiler_params(2),
        name="pre",
    )(x, *weights, tabs["cos_m"], tabs["sin_m"], tabs["cos_a"], tabs["sin_a"])


def _column_max(x):
    rows = x.shape[0]
    while rows > SUBLANES:
        rows //= 2
        x = jnp.maximum(x[:rows], x[rows:])
    return jnp.max(x, axis=0, keepdims=True)


def _attn_kernel(q_ref, k_ref, vt_ref, o_ref, m_ref, acc_ref, s_ref, cm_ref, qt_ref, *, n_heads,
                 k_tiles, v_rows, n_chunks, n_tiles, tq, tk):
    def reset():
        m_ref[...] = jnp.full(m_ref.shape, -jnp.inf, F32)
        acc_ref[...] = jnp.zeros(acc_ref.shape, F32)

    def scores(t, j, slot, h):
        start = pl.multiple_of(j * tk, tk)
        kt = k_tiles[h]
        qht = qt_ref[t, h * LANES:(h + 1) * LANES, :]
        kc = k_ref[0, pl.ds(start, tk), kt * LANES:(kt + 1) * LANES]
        st = _dot(kc, qht)
        s_ref[slot, h] = st
        cm_ref[slot, h] = _column_max(st)

    def accumulate(j, slot, h):
        vr = v_rows[h]
        m_old = m_ref[h]
        m_new = jnp.maximum(m_old, cm_ref[slot, h])
        alpha = jnp.exp2(m_old - m_new)
        p = jnp.exp2(s_ref[slot, h] - m_new).astype(BF16)
        vc = vt_ref[0, j, vr * V_ROWS:(vr + 1) * V_ROWS, :]
        acc_ref[h] = alpha * acc_ref[h] + _dot(vc, p)
        m_ref[h] = m_new

    def step(t, j, slot, t_next, j_next):
        for h in range(n_heads):
            scores(t_next, j_next, 1 - slot, h)
            accumulate(j, slot, h)

    def tile_body(t, carry):
        def chunks_body(jb, c):
            j0 = jb * ATTN_UNROLL
            for u in range(ATTN_UNROLL - 1):
                step(t, j0 + u, u % 2, t, j0 + u + 1)
            j_last = j0 + ATTN_UNROLL - 1
            wrap = (j_last + 1 == n_chunks).astype(jnp.int32)
            step(t, j_last, (ATTN_UNROLL - 1) % 2,
                 jnp.minimum(t + wrap, n_tiles - 1), (1 - wrap) * (j_last + 1))
            return c

        lax.fori_loop(0, n_chunks // ATTN_UNROLL, chunks_body, 0)
        outs = []
        for h in range(n_heads):
            acc = acc_ref[h]
            outs.append(acc[:HEAD_DIM] / acc[HEAD_DIM:HEAD_DIM + 1])
        o_ref[0, pl.ds(pl.multiple_of(t * tq, tq), tq), :] = jnp.concatenate(outs, axis=0).T
        reset()
        return carry

    for t in range(n_tiles):
        qt_ref[t] = q_ref[0, t * tq:(t + 1) * tq, :].astype(F32).T.astype(BF16)
    reset()
    for h in range(n_heads):
        scores(0, 0, 0, h)
    lax.fori_loop(0, n_tiles, tile_body, 0)


def _attn_call(q, k, vt, *, k_tiles, v_rows, name):
    B, S, qw = q.shape
    n_heads = qw // LANES
    tq, tk, n_tiles = ATTN_TQ, ATTN_TK, ATTN_TILES
    n_chunks = S // tk
    rows = tq * n_tiles
    assert S % rows == 0 and S % tk == 0 and n_chunks % ATTN_UNROLL == 0, (S, rows, tk)
    assert ATTN_UNROLL % 2 == 0
    kw = k.shape[2]
    vw = vt.shape[2]
    ow = n_heads * HEAD_DIM
    kv_bytes = S * (kw + vw) * 2
    other_bytes = 2 * rows * (qw * 2 + ow * 4) + 2 * n_heads * tk * tq * 4
    kv_buffers = 2 if 2 * kv_bytes + other_bytes < ATTN_VMEM_BUDGET_BYTES else 1
    kern = functools.partial(_attn_kernel, n_heads=n_heads, k_tiles=k_tiles, v_rows=v_rows,
                             n_chunks=n_chunks, n_tiles=n_tiles, tq=tq, tk=tk)
    return pl.pallas_call(
        kern,
        grid=(B, S // rows),
        in_specs=[
            pl.BlockSpec((1, rows, qw), lambda b, i: (b, i, 0)),
            pl.BlockSpec((1, S, kw), lambda b, i: (b, 0, 0), pipeline_mode=pl.Buffered(kv_buffers)),
            pl.BlockSpec((1, n_chunks, vw, tk), lambda b, i: (b, 0, 0, 0),
                         pipeline_mode=pl.Buffered(kv_buffers)),
        ],
        out_specs=pl.BlockSpec((1, rows, ow), lambda b, i: (b, i, 0)),
        out_shape=jax.ShapeDtypeStruct((B, S, ow), F32),
        scratch_shapes=[pltpu.VMEM((n_heads, 1, tq), F32),
                        pltpu.VMEM((n_heads, V_ROWS, tq), F32),
                        pltpu.VMEM((2, n_heads, tk, tq), F32),
                        pltpu.VMEM((2, n_heads, 1, tq), F32),
                        pltpu.VMEM((n_tiles, qw, tq), BF16)],
        compiler_params=_compiler_params(2),
        name=name,
    )(q, k, vt)


def _na_kernel(na_ref, bias_ref, o_ref, s_ref, *, rows, rb):
    i = pl.program_id(1)
    lane = lax.broadcasted_iota(jnp.int32, (GRID_W, LANES), 1)
    first = lane < HEAD_DIM
    n_pairs = NA_HEADS // 2

    def window_start(u):
        r = i * rb + u
        rs = jnp.clip(r - NA_ROWS // 2, 0, rows - NA_ROWS)
        return r, rs

    def scores(u, t):
        r, rs = window_start(u)
        kv0 = pl.multiple_of(rs * GRID_W, GRID_W)
        qt = na_ref[0, pl.ds(pl.multiple_of(r * GRID_W, GRID_W), GRID_W), t * LANES:(t + 1) * LANES]
        kt = na_ref[0, pl.ds(kv0, NA_KEYS), NA_D + t * LANES:NA_D + (t + 1) * LANES]
        zero = jnp.zeros_like(qt)
        qm = jnp.concatenate([jnp.where(first, qt, zero), jnp.where(first, zero, qt)], axis=0)
        s_ref[u % 2, t] = _dot_nt(qm, kt) + bias_ref[t, r - rs]

    def finish(u, t):
        _, rs = window_start(u)
        kv0 = pl.multiple_of(rs * GRID_W, GRID_W)
        vt = na_ref[0, pl.ds(kv0, NA_KEYS), 2 * NA_D + t * LANES:2 * NA_D + (t + 1) * LANES]
        s = s_ref[u % 2, t]
        m = jnp.max(s, axis=-1, keepdims=True)
        p = jnp.exp(s - m)
        l = jnp.sum(p, axis=-1, keepdims=True)
        o2 = _dot(p.astype(BF16), vt) / l
        o_ref[0, u * GRID_W:(u + 1) * GRID_W, t * LANES:(t + 1) * LANES] = (
            jnp.where(first, o2[:GRID_W], o2[GRID_W:]))

    for t in range(n_pairs):
        scores(0, t)
    for u in range(rb):
        if u + 1 < rb:
            for t in range(n_pairs):
                scores(u + 1, t)
        for t in range(n_pairs):
            finish(u, t)


def _na_call(na, bias):
    B, S, w = na.shape
    rows = S // GRID_W
    rb = NA_RB
    kern = functools.partial(_na_kernel, rows=rows, rb=rb)
    return pl.pallas_call(
        kern,
        grid=(B, rows // rb),
        in_specs=[
            pl.BlockSpec((1, S, w), lambda b, i: (b, 0, 0)),
            _const_spec(bias.shape),
        ],
        out_specs=pl.BlockSpec((1, rb * GRID_W, NA_D), lambda b, i: (b, i, 0)),
        out_shape=jax.ShapeDtypeStruct((B, S, NA_D), F32),
        scratch_shapes=[pltpu.VMEM((2, NA_HEADS // 2, 2 * GRID_W, NA_KEYS), F32)],
        compiler_params=_compiler_params(2),
        name="na",
    )(na, bias)


def _post_kernel(x_ref, oa_ref, ob_ref, oc_ref, ga_ref, gb_ref, gc_ref, wout_ref, gffn_ref,
                 wg_ref, wu_ref, wd_ref, gfin_ref, o_ref, *, final):
    merged = jnp.concatenate(
        [_rms(oa_ref[...], ga_ref[...]), _rms(ob_ref[...], gb_ref[...]),
         _rms(oc_ref[...], gc_ref[...])], axis=-1).astype(BF16)
    x1 = x_ref[...] + _dot(merged, wout_ref[...])
    hb = _rms(x1, gffn_ref[...]).astype(BF16)
    g = _dot(hb, wg_ref[...])
    u = _dot(hb, wu_ref[...])
    act = (g * jax.nn.sigmoid(g) * u).astype(BF16)
    x2 = x1 + _dot(act, wd_ref[...])
    if final:
        x2 = _rms(x2, gfin_ref[...])
    o_ref[...] = x2


def _post_call(x, oa, ob, oc, lw, g_final, final):
    B, S, _ = x.shape
    T = B * S
    tm = POST_TM
    flat = lambda a: a.reshape(T, a.shape[-1])
    tile = lambda w: pl.BlockSpec((tm, w), lambda i: (i, 0))
    weights_a = [lw["g_a"], lw["g_b"], lw["g_c"], lw["w_out"], lw["g_ffn"],
                 lw["w_gate"], lw["w_up"], lw["w_down"], g_final]
    out = pl.pallas_call(
        functools.partial(_post_kernel, final=final),
        grid=(T // tm,),
        in_specs=[tile(D_MODEL), tile(NA_D), tile(MLA_HEADS * MLA_V), tile(GQA_HEADS * HEAD_DIM)]
        + [_const_spec(w.shape) for w in weights_a],
        out_specs=tile(D_MODEL),
        out_shape=jax.ShapeDtypeStruct((T, D_MODEL), F32),
        compiler_params=_compiler_params(1),
        name="post",
    )(flat(x), flat(oa), flat(ob), flat(oc), *weights_a)
    return out.reshape(B, S, D_MODEL)


def _rot32(w):
    return jnp.concatenate([-w[:, 16:32], w[:, 0:16]], axis=1)


def _rot64(w):
    return jnp.concatenate([_rot32(w[:, :32]), _rot32(w[:, 32:])], axis=1)


def _perm64(g):
    p = lambda v: jnp.concatenate([v[16:32], v[0:16]])
    return jnp.concatenate([p(g[:32]), p(g[32:])])


def _layer_weights(l, attn_norm, w_in, q_a_norm, kv_a_norm, w_uq, w_ukv, q_norm_c, k_norm_c,
                   g_out_a, g_out_b, g_out_c, w_out, ffn_norm, w_gate, w_up, w_down):
    wi = w_in[l]
    K = wi.shape[0]
    sizes = (NA_D, NA_D, NA_D, MLA_Q_RANK, MLA_KV_RANK, MLA_ROPE,
             GQA_HEADS * HEAD_DIM, GQA_KV_HEADS * HEAD_DIM, GQA_KV_HEADS * HEAD_DIM)
    offs = np.concatenate([[0], np.cumsum(sizes)])
    (wqa, wka, wva, wcq, wckv, wkpe, wqc, wkc, wvc) = [wi[:, offs[i]:offs[i + 1]] for i in range(9)]
    z = lambda n: jnp.zeros((K, n), F32)

    w_na = jnp.concatenate([wqa * NA_QSCALE, wka, wva], axis=1)
    kpe_tile = jnp.concatenate([z(MLA_NOPE), wkpe, z(LANES - MLA_QK)], axis=1)
    kpe_rot = jnp.concatenate([z(MLA_NOPE), _rot32(wkpe), z(LANES - MLA_QK)], axis=1)
    w_mla = jnp.concatenate([wcq, wckv, kpe_tile, kpe_rot], axis=1)

    qc_tiles, qc_rots = [], []
    for h in range(GQA_HEADS):
        wh = wqc[:, h * HEAD_DIM:(h + 1) * HEAD_DIM]
        pad = (lambda a: jnp.concatenate([a, z(HEAD_DIM)], axis=1)) if h // GQA_GROUP == 0 else \
              (lambda a: jnp.concatenate([z(HEAD_DIM), a], axis=1))
        qc_tiles.append(pad(wh))
        qc_rots.append(pad(_rot64(wh)))
    w_qc = jnp.concatenate(qc_tiles + qc_rots, axis=1)
    w_kc = jnp.concatenate(
        [wkc] + [_rot64(wkc[:, h * HEAD_DIM:(h + 1) * HEAD_DIM]) for h in range(GQA_KV_HEADS)], axis=1)

    uq = w_uq[l]
    zq = lambda n: jnp.zeros((MLA_Q_RANK, n), F32)
    uq_tiles, uq_rots = [], []
    for h in range(MLA_HEADS):
        wh = uq[:, h * MLA_QK:(h + 1) * MLA_QK]
        uq_tiles.append(jnp.concatenate([wh, zq(LANES - MLA_QK)], axis=1))
        uq_rots.append(jnp.concatenate([zq(MLA_NOPE), _rot32(wh[:, MLA_NOPE:]), zq(LANES - MLA_QK)], axis=1))
    w_uq_ext = jnp.concatenate(uq_tiles + uq_rots, axis=1)

    ukv = w_ukv[l]
    zk = jnp.zeros((MLA_KV_RANK, LANES - MLA_NOPE), F32)
    per = MLA_NOPE + MLA_V
    w_uk = jnp.concatenate(
        [jnp.concatenate([ukv[:, h * per:h * per + MLA_NOPE], zk], axis=1) for h in range(MLA_HEADS)], axis=1)
    w_uv = jnp.concatenate([ukv[:, h * per + MLA_NOPE:(h + 1) * per] for h in range(MLA_HEADS)], axis=1)

    two = lambda g: jnp.concatenate([g, g])
    row = lambda g: g.reshape(1, -1).astype(F32)
    return {
        "g_attn": row(attn_norm[l]),
        "w_na": w_na.astype(BF16), "w_mla": w_mla.astype(BF16), "w_qc": w_qc.astype(BF16),
        "w_kc": w_kc.astype(BF16), "w_vc": wvc.astype(BF16),
        "g_qa": row(q_a_norm[l]), "g_kva": row(kv_a_norm[l]),
        "w_uq": w_uq_ext.astype(BF16), "w_uk": w_uk.astype(BF16), "w_uv": w_uv.astype(BF16),
        "g_q": jnp.stack([two(q_norm_c[l]), two(_perm64(q_norm_c[l]))]).astype(F32),
        "g_k": jnp.stack([two(k_norm_c[l]), two(_perm64(k_norm_c[l]))]).astype(F32),
        "g_a": row(g_out_a[l]), "g_b": row(g_out_b[l]), "g_c": row(g_out_c[l]),
        "w_out": w_out[l].astype(BF16), "g_ffn": row(ffn_norm[l]),
        "w_gate": w_gate[l].astype(BF16), "w_up": w_up[l].astype(BF16), "w_down": w_down[l].astype(BF16),
    }


def _rope_tables(S):
    tok = jnp.arange(S)
    t = tok.astype(F32)
    row = (tok // GRID_W).astype(F32)
    col = (tok % GRID_W).astype(F32)
    half = MLA_ROPE // 2
    inv = ROPE_BASE ** (-(jnp.arange(half, dtype=F32) * 2.0) / MLA_ROPE)

    def cs(pos):
        ang = pos[:, None] * inv[None, :]
        c, s = jnp.cos(ang), jnp.sin(ang)
        return jnp.concatenate([c, c], axis=1), jnp.concatenate([s, s], axis=1)

    ct, st = cs(t)
    ones = jnp.ones((S, MLA_NOPE), F32)
    zeros = jnp.zeros((S, MLA_NOPE), F32)
    pad1 = jnp.ones((S, LANES - MLA_QK), F32)
    pad0 = jnp.zeros((S, LANES - MLA_QK), F32)
    cr, sr = cs(row)
    cc, sc = cs(col)
    ca = jnp.concatenate([cr, cc], axis=1)
    sa = jnp.concatenate([sr, sc], axis=1)
    return {
        "cos_m": jnp.concatenate([ones, ct, pad1], axis=1),
        "sin_m": jnp.concatenate([zeros, st, pad0], axis=1),
        "cos_a": jnp.concatenate([ca, ca], axis=1),
        "sin_a": jnp.concatenate([sa, sa], axis=1),
    }


def _na_bias_table(rpb_l):
    c = np.arange(GRID_W)
    cstart = np.clip(c - NA_COLS // 2, 0, GRID_W - NA_COLS)
    cp = np.arange(GRID_W)
    dc = cp[None, :] - c[:, None] + (NA_COLS - 1)
    valid = (cp[None, :] >= cstart[:, None]) & (cp[None, :] < cstart[:, None] + NA_COLS)
    onehot = ((dc[:, :, None] == np.arange(2 * NA_COLS - 1)) & valid[:, :, None]).astype(np.float32)
    t = jnp.einsum("hrd,cpd->hrcp", rpb_l.astype(F32), onehot, precision=lax.Precision.HIGHEST)
    t = jnp.where(valid[None, None], t, MASK_VALUE)
    b = jnp.stack([t[:, NA_ROWS - 1 - d:2 * NA_ROWS - 1 - d] for d in range(NA_ROWS)], axis=1)
    b = b.transpose(0, 1, 3, 2, 4).reshape(NA_HEADS // 2, 2, NA_ROWS, GRID_W, NA_KEYS)
    return b.transpose(0, 2, 1, 3, 4).reshape(NA_HEADS // 2, NA_ROWS, 2 * GRID_W, NA_KEYS)


MLA_K_TILES = tuple(range(MLA_HEADS))
MLA_V_ROWS = tuple(range(MLA_HEADS))
GQA_K_TILES = (0,) * GQA_HEADS
GQA_V_ROWS = tuple(h // GQA_GROUP for h in range(GQA_HEADS))


def _trunk(x, layers, biases, tabs, g_final):
    n_layers = len(layers)
    for l, lw in enumerate(layers):
        na, mq, mk, mvt, gq, gk, gvt = _pre_call(x, lw, tabs)
        oa = _na_call(na, biases[l])
        ob = _attn_call(mq, mk, mvt, k_tiles=MLA_K_TILES, v_rows=MLA_V_ROWS, name="mla")
        oc = _attn_call(gq, gk, gvt, k_tiles=GQA_K_TILES, v_rows=GQA_V_ROWS, name="gqa")
        x = _post_call(x, oa, ob, oc, lw, g_final, final=(l == n_layers - 1))
    return x


def kernel(x_prompt, x_sample, attn_norm, w_in, q_a_norm, kv_a_norm, w_uq, w_ukv, q_norm_c, k_norm_c, rpb, g_out_a, g_out_b, g_out_c, w_out, ffn_norm, w_gate, w_up, w_down, final_norm):
    depth = w_in.shape[0]
    layers = [_layer_weights(l, attn_norm, w_in, q_a_norm, kv_a_norm, w_uq, w_ukv, q_norm_c,
                             k_norm_c, g_out_a, g_out_b, g_out_c, w_out, ffn_norm, w_gate,
                             w_up, w_down) for l in range(depth)]
    biases = [_na_bias_table(rpb[l]) for l in range(depth)]
    g_final = final_norm.reshape(1, -1).astype(F32)
    outs = []
    for x in (x_prompt, x_sample):
        tabs = _rope_tables(x.shape[1])
        outs.append(_trunk(x, layers, biases, tabs, g_final))
    return tuple(outs)
```

```python
import functools
import math

import jax
import jax.numpy as jnp
import numpy as np
from jax import lax
from jax.experimental import pallas as pl
from jax.experimental.pallas import tpu as pltpu

D_MODEL = 1024
GRID_W = 64
HEAD_DIM = 64
NA_HEADS = 4
NA_ROWS = 8
NA_COLS = 16
NA_D = NA_HEADS * HEAD_DIM
MLA_HEADS = 6
MLA_Q_RANK = 384
MLA_KV_RANK = 256
MLA_NOPE = 64
MLA_ROPE = 32
MLA_V = 64
MLA_QK = MLA_NOPE + MLA_ROPE
GQA_HEADS = 6
GQA_KV_HEADS = 2
GQA_GROUP = GQA_HEADS // GQA_KV_HEADS
D_FF = 2816
ROPE_BASE = 10000.0
EPS = 1e-6

LANES = 128
SUBLANES = 8
VMEM_LIMIT_BYTES = 56 * 1024 * 1024
ATTN_VMEM_BUDGET_BYTES = 48 * 1024 * 1024

PRE_TM = 512
ATTN_TQ = 512
ATTN_TK = PRE_TM
ATTN_TILES = 4
ATTN_UNROLL = 4
ONES_ROWS = 16
V_ROWS = HEAD_DIM + ONES_ROWS
NA_RB = 16
NA_KEYS = NA_ROWS * GRID_W
POST_TM = 512

LOG2E = 1.4426950408889634
MLA_QSCALE = LOG2E / math.sqrt(MLA_QK)
GQA_QSCALE = LOG2E / math.sqrt(HEAD_DIM)
NA_QSCALE = 1.0 / math.sqrt(HEAD_DIM)
MASK_VALUE = -1e30

BF16 = jnp.bfloat16
F32 = jnp.float32
NT_DIMS = (((1,), (1,)), ((), ()))


def _dot(a, b):
    return jnp.dot(a, b, preferred_element_type=F32)


def _dot_nt(a, b):
    return lax.dot_general(a, b, NT_DIMS, preferred_element_type=F32)


def _rms(x, g):
    return x * lax.rsqrt(jnp.mean(x * x, axis=-1, keepdims=True) + EPS) * g


def _with_ones_rows(vt, n_heads):
    ones = jnp.ones((ONES_ROWS, vt.shape[1]), F32)
    parts = []
    for h in range(n_heads):
        parts += [vt[h * HEAD_DIM:(h + 1) * HEAD_DIM], ones]
    return jnp.concatenate(parts, axis=0)


def _compiler_params(n_axes):
    return pltpu.CompilerParams(
        dimension_semantics=("arbitrary",) * n_axes,
        vmem_limit_bytes=VMEM_LIMIT_BYTES,
    )


def _const_spec(shape):
    nd = len(shape)
    return pl.BlockSpec(shape, lambda *_: (0,) * nd, pipeline_mode=pl.Buffered(1))


def _pre_kernel(x_ref, gattn_ref, wna_ref, wmla_ref, wqc_ref, wkc_ref, wvc_ref,
                qan_ref, kvan_ref, wuq_ref, wuk_ref, wuv_ref, gq_ref, gk_ref,
                cosm_ref, sinm_ref, cosa_ref, sina_ref,
                na_ref, mq_ref, mk_ref, mvt_ref, gqo_ref, gko_ref, gvt_ref):
    x = x_ref[0]
    xb = _rms(x, gattn_ref[...]).astype(BF16)

    na_ref[0] = _dot(xb, wna_ref[...]).astype(BF16)

    cosm = cosm_ref[...]
    sinm = sinm_ref[...]
    cosa = cosa_ref[...]
    sina = sina_ref[...]

    lat = _dot(xb, wmla_ref[...])
    cq = lat[:, :MLA_Q_RANK]
    ckv = lat[:, MLA_Q_RANK:MLA_Q_RANK + MLA_KV_RANK]
    o = MLA_Q_RANK + MLA_KV_RANK
    kpe = lat[:, o:o + LANES] * cosm + lat[:, o + LANES:o + 2 * LANES] * sinm

    cqn = _rms(cq, qan_ref[...]).astype(BF16)
    q2 = _dot(cqn, wuq_ref[...])
    hw = MLA_HEADS * LANES
    for h in range(MLA_HEADS):
        sl = slice(h * LANES, (h + 1) * LANES)
        qh = q2[:, sl] * cosm + q2[:, hw + h * LANES:hw + (h + 1) * LANES] * sinm
        mq_ref[0, :, sl] = (qh * MLA_QSCALE).astype(BF16)

    ckvn = _rms(ckv, kvan_ref[...]).astype(BF16)
    kn = _dot(ckvn, wuk_ref[...])
    for h in range(MLA_HEADS):
        sl = slice(h * LANES, (h + 1) * LANES)
        mk_ref[0, :, sl] = (kn[:, sl] + kpe).astype(BF16)
    mvt_ref[0, 0] = _with_ones_rows(_dot(ckvn, wuv_ref[...]).T, MLA_HEADS).astype(BF16)

    gq = gq_ref[...]
    gqc = cosa * gq[0:1]
    gqs = sina * gq[1:2]
    qc2 = _dot(xb, wqc_ref[...])
    hw = GQA_HEADS * LANES
    for h in range(GQA_HEADS):
        sl = slice(h * LANES, (h + 1) * LANES)
        xh = qc2[:, sl]
        xr = qc2[:, hw + h * LANES:hw + (h + 1) * LANES]
        ms = jnp.sum(xh * xh, axis=-1, keepdims=True) * (1.0 / HEAD_DIM)
        sc = lax.rsqrt(ms + EPS) * GQA_QSCALE
        gqo_ref[0, :, sl] = ((xh * gqc + xr * gqs) * sc).astype(BF16)

    gk = gk_ref[...]
    kc2 = _dot(xb, wkc_ref[...])
    xk = kc2[:, :LANES]
    xkr = kc2[:, LANES:]
    lane = lax.broadcasted_iota(jnp.int32, xk.shape, 1)
    first = lane < HEAD_DIM
    sq = xk * xk
    ms0 = jnp.sum(jnp.where(first, sq, 0.0), axis=-1, keepdims=True) * (1.0 / HEAD_DIM)
    ms1 = jnp.sum(jnp.where(first, 0.0, sq), axis=-1, keepdims=True) * (1.0 / HEAD_DIM)
    sc = jnp.where(first, lax.rsqrt(ms0 + EPS), lax.rsqrt(ms1 + EPS))
    gko_ref[0] = ((xk * (cosa * gk[0:1]) + xkr * (sina * gk[1:2])) * sc).astype(BF16)
    gvt_ref[0, 0] = _with_ones_rows(_dot(xb, wvc_ref[...]).T, GQA_KV_HEADS).astype(BF16)


def _pre_call(x, lw, tabs):
    B, S, _ = x.shape
    tm = PRE_TM
    nt = S // tm
    tile = lambda w: pl.BlockSpec((1, tm, w), lambda b, i: (b, i, 0))
    tab = pl.BlockSpec((tm, LANES), lambda b, i: (i, 0))
    weights = [lw["g_attn"], lw["w_na"], lw["w_mla"], lw["w_qc"], lw["w_kc"], lw["w_vc"],
               lw["g_qa"], lw["g_kva"], lw["w_uq"], lw["w_uk"], lw["w_uv"], lw["g_q"], lw["g_k"]]
    out_shape = (
        jax.ShapeDtypeStruct((B, S, 3 * NA_D), BF16),
        jax.ShapeDtypeStruct((B, S, MLA_HEADS * LANES), BF16),
        jax.ShapeDtypeStruct((B, S, MLA_HEADS * LANES), BF16),
        jax.ShapeDtypeStruct((B, nt, MLA_HEADS * V_ROWS, tm), BF16),
        jax.ShapeDtypeStruct((B, S, GQA_HEADS * LANES), BF16),
        jax.ShapeDtypeStruct((B, S, GQA_KV_HEADS * HEAD_DIM), BF16),
        jax.ShapeDtypeStruct((B, nt, GQA_KV_HEADS * V_ROWS, tm), BF16),
    )
    out_specs = (
        tile(3 * NA_D), tile(MLA_HEADS * LANES), tile(MLA_HEADS * LANES),
        pl.BlockSpec((1, 1, MLA_HEADS * V_ROWS, tm), lambda b, i: (b, i, 0, 0)),
        tile(GQA_HEADS * LANES), tile(GQA_KV_HEADS * HEAD_DIM),
        pl.BlockSpec((1, 1, GQA_KV_HEADS * V_ROWS, tm), lambda b, i: (b, i, 0, 0)),
    )
    return pl.pallas_call(
        _pre_kernel,
        grid=(B, nt),
        in_specs=[tile(D_MODEL)] + [_const_spec(w.shape) for w in weights] + [tab] * 4,
        out_specs=out_specs,
        out_shape=out_shape,
        compiler_params=_compiler_params(2),
        name="pre",
    )(x, *weights, tabs["cos_m"], tabs["sin_m"], tabs["cos_a"], tabs["sin_a"])


def _column_max(x):
    rows = x.shape[0]
    while rows > SUBLANES:
        rows //= 2
        x = jnp.maximum(x[:rows], x[rows:])
    return jnp.max(x, axis=0, keepdims=True)


def _attn_kernel(q_ref, k_ref, vt_ref, o_ref, m_ref, acc_ref, s_ref, cm_ref, qt_ref, *, n_heads,
                 k_tiles, v_rows, n_chunks, n_tiles, tq, tk):
    def reset():
        m_ref[...] = jnp.full(m_ref.shape, -jnp.inf, F32)
        acc_ref[...] = jnp.zeros(acc_ref.shape, F32)

    def scores(t, j, slot, h):
        start = pl.multiple_of(j * tk, tk)
        kt = k_tiles[h]
        qht = qt_ref[t, h * LANES:(h + 1) * LANES, :]
        kc = k_ref[0, pl.ds(start, tk), kt * LANES:(kt + 1) * LANES]
        st = _dot(kc, qht)
        s_ref[slot, h] = st
        cm_ref[slot, h] = _column_max(st)

    def accumulate(j, slot, h):
        vr = v_rows[h]
        m_old = m_ref[h]
        m_new = jnp.maximum(m_old, cm_ref[slot, h])
        alpha = jnp.exp2(m_old - m_new)
        p = jnp.exp2(s_ref[slot, h] - m_new).astype(BF16)
        vc = vt_ref[0, j, vr * V_ROWS:(vr + 1) * V_ROWS, :]
        acc_ref[h] = alpha * acc_ref[h] + _dot(vc, p)
        m_ref[h] = m_new

    def step(t, j, slot, t_next, j_next):
        for h in range(n_heads):
            scores(t_next, j_next, 1 - slot, h)
            accumulate(j, slot, h)

    def tile_body(t, carry):
        def chunks_body(jb, c):
            j0 = jb * ATTN_UNROLL
            for u in range(ATTN_UNROLL - 1):
                step(t, j0 + u, u % 2, t, j0 + u + 1)
            j_last = j0 + ATTN_UNROLL - 1
            wrap = (j_last + 1 == n_chunks).astype(jnp.int32)
            step(t, j_last, (ATTN_UNROLL - 1) % 2,
                 jnp.minimum(t + wrap, n_tiles - 1), (1 - wrap) * (j_last + 1))
            return c

        lax.fori_loop(0, n_chunks // ATTN_UNROLL, chunks_body, 0)
        outs = []
        for h in range(n_heads):
            acc = acc_ref[h]
            outs.append(acc[:HEAD_DIM] / acc[HEAD_DIM:HEAD_DIM + 1])
        o_ref[0, pl.ds(pl.multiple_of(t * tq, tq), tq), :] = jnp.concatenate(outs, axis=0).T
        reset()
        return carry

    for t in range(n_tiles):
        qt_ref[t] = q_ref[0, t * tq:(t + 1) * tq, :].astype(F32).T.astype(BF16)
    reset()
    for h in range(n_heads):
        scores(0, 0, 0, h)
    lax.fori_loop(0, n_tiles, tile_body, 0)


def _attn_call(q, k, vt, *, k_tiles, v_rows, name):
    B, S, qw = q.shape
    n_heads = qw // LANES
    tq, tk, n_tiles = ATTN_TQ, ATTN_TK, ATTN_TILES
    n_chunks = S // tk
    rows = tq * n_tiles
    assert S % rows == 0 and S % tk == 0 and n_chunks % ATTN_UNROLL == 0, (S, rows, tk)
    assert ATTN_UNROLL % 2 == 0
    kw = k.shape[2]
    vw = vt.shape[2]
    ow = n_heads * HEAD_DIM
    kv_bytes = S * (kw + vw) * 2
    other_bytes = 2 * rows * (qw * 2 + ow * 4) + 2 * n_heads * tk * tq * 4
    kv_buffers = 2 if 2 * kv_bytes + other_bytes < ATTN_VMEM_BUDGET_BYTES else 1
    kern = functools.partial(_attn_kernel, n_heads=n_heads, k_tiles=k_tiles, v_rows=v_rows,
                             n_chunks=n_chunks, n_tiles=n_tiles, tq=tq, tk=tk)
    return pl.pallas_call(
        kern,
        grid=(B, S // rows),
        in_specs=[
            pl.BlockSpec((1, rows, qw), lambda b, i: (b, i, 0)),
            pl.BlockSpec((1, S, kw), lambda b, i: (b, 0, 0), pipeline_mode=pl.Buffered(kv_buffers)),
            pl.BlockSpec((1, n_chunks, vw, tk), lambda b, i: (b, 0, 0, 0),
                         pipeline_mode=pl.Buffered(kv_buffers)),
        ],
        out_specs=pl.BlockSpec((1, rows, ow), lambda b, i: (b, i, 0)),
        out_shape=jax.ShapeDtypeStruct((B, S, ow), F32),
        scratch_shapes=[pltpu.VMEM((n_heads, 1, tq), F32),
                        pltpu.VMEM((n_heads, V_ROWS, tq), F32),
                        pltpu.VMEM((2, n_heads, tk, tq), F32),
                        pltpu.VMEM((2, n_heads, 1, tq), F32),
                        pltpu.VMEM((n_tiles, qw, tq), BF16)],
        compiler_params=_compiler_params(2),
        name=name,
    )(q, k, vt)


def _na_kernel(na_ref, bias_ref, o_ref, s_ref, *, rows, rb):
    i = pl.program_id(1)
    lane = lax.broadcasted_iota(jnp.int32, (GRID_W, LANES), 1)
    first = lane < HEAD_DIM
    n_pairs = NA_HEADS // 2

    def window_start(u):
        r = i * rb + u
        rs = jnp.clip(r - NA_ROWS // 2, 0, rows - NA_ROWS)
        return r, rs

    def scores(u, t):
        r, rs = window_start(u)
        kv0 = pl.multiple_of(rs * GRID_W, GRID_W)
        qt = na_ref[0, pl.ds(pl.multiple_of(r * GRID_W, GRID_W), GRID_W), t * LANES:(t + 1) * LANES]
        kt = na_ref[0, pl.ds(kv0, NA_KEYS), NA_D + t * LANES:NA_D + (t + 1) * LANES]
        zero = jnp.zeros_like(qt)
        qm = jnp.concatenate([jnp.where(first, qt, zero), jnp.where(first, zero, qt)], axis=0)
        s_ref[u % 2, t] = _dot_nt(qm, kt) + bias_ref[t, r - rs]

    def finish(u, t):
        _, rs = window_start(u)
        kv0 = pl.multiple_of(rs * GRID_W, GRID_W)
        vt = na_ref[0, pl.ds(kv0, NA_KEYS), 2 * NA_D + t * LANES:2 * NA_D + (t + 1) * LANES]
        s = s_ref[u % 2, t]
        m = jnp.max(s, axis=-1, keepdims=True)
        p = jnp.exp(s - m)
        l = jnp.sum(p, axis=-1, keepdims=True)
        o2 = _dot(p.astype(BF16), vt) / l
        o_ref[0, u * GRID_W:(u + 1) * GRID_W, t * LANES:(t + 1) * LANES] = (
            jnp.where(first, o2[:GRID_W], o2[GRID_W:]))

    for t in range(n_pairs):
        scores(0, t)
    for u in range(rb):
        if u + 1 < rb:
            for t in range(n_pairs):
                scores(u + 1, t)
        for t in range(n_pairs):
            finish(u, t)


def _na_call(na, bias):
    B, S, w = na.shape
    rows = S // GRID_W
    rb = NA_RB
    kern = functools.partial(_na_kernel, rows=rows, rb=rb)
    return pl.pallas_call(
        kern,
        grid=(B, rows // rb),
        in_specs=[
            pl.BlockSpec((1, S, w), lambda b, i: (b, 0, 0)),
            _const_spec(bias.shape),
        ],
        out_specs=pl.BlockSpec((1, rb * GRID_W, NA_D), lambda b, i: (b, i, 0)),
        out_shape=jax.ShapeDtypeStruct((B, S, NA_D), F32),
        scratch_shapes=[pltpu.VMEM((2, NA_HEADS // 2, 2 * GRID_W, NA_KEYS), F32)],
        compiler_params=_compiler_params(2),
        name="na",
    )(na, bias)


def _post_kernel(x_ref, oa_ref, ob_ref, oc_ref, ga_ref, gb_ref, gc_ref, wout_ref, gffn_ref,
                 wg_ref, wu_ref, wd_ref, gfin_ref, o_ref, *, final):
    merged = jnp.concatenate(
        [_rms(oa_ref[...], ga_ref[...]), _rms(ob_ref[...], gb_ref[...]),
         _rms(oc_ref[...], gc_ref[...])], axis=-1).astype(BF16)
    x1 = x_ref[...] + _dot(merged, wout_ref[...])
    hb = _rms(x1, gffn_ref[...]).astype(BF16)
    g = _dot(hb, wg_ref[...])
    u = _dot(hb, wu_ref[...])
    act = (g * jax.nn.sigmoid(g) * u).astype(BF16)
    x2 = x1 + _dot(act, wd_ref[...])
    if final:
        x2 = _rms(x2, gfin_ref[...])
    o_ref[...] = x2


def _post_call(x, oa, ob, oc, lw, g_final, final):
    B, S, _ = x.shape
    T = B * S
    tm = POST_TM
    flat = lambda a: a.reshape(T, a.shape[-1])
    tile = lambda w: pl.BlockSpec((tm, w), lambda i: (i, 0))
    weights_a = [lw["g_a"], lw["g_b"], lw["g_c"], lw["w_out"], lw["g_ffn"],
                 lw["w_gate"], lw["w_up"], lw["w_down"], g_final]
    out = pl.pallas_call(
        functools.partial(_post_kernel, final=final),
        grid=(T // tm,),
        in_specs=[tile(D_MODEL), tile(NA_D), tile(MLA_HEADS * MLA_V), tile(GQA_HEADS * HEAD_DIM)]
        + [_const_spec(w.shape) for w in weights_a],
        out_specs=tile(D_MODEL),
        out_shape=jax.ShapeDtypeStruct((T, D_MODEL), F32),
        compiler_params=_compiler_params(1),
        name="post",
    )(flat(x), flat(oa), flat(ob), flat(oc), *weights_a)
    return out.reshape(B, S, D_MODEL)


def _rot32(w):
    return jnp.concatenate([-w[:, 16:32], w[:, 0:16]], axis=1)


def _rot64(w):
    return jnp.concatenate([_rot32(w[:, :32]), _rot32(w[:, 32:])], axis=1)


def _perm64(g):
    p = lambda v: jnp.concatenate([v[16:32], v[0:16]])
    return jnp.concatenate([p(g[:32]), p(g[32:])])


def _layer_weights(l, attn_norm, w_in, q_a_norm, kv_a_norm, w_uq, w_ukv, q_norm_c, k_norm_c,
                   g_out_a, g_out_b, g_out_c, w_out, ffn_norm, w_gate, w_up, w_down):
    wi = w_in[l]
    K = wi.shape[0]
    sizes = (NA_D, NA_D, NA_D, MLA_Q_RANK, MLA_KV_RANK, MLA_ROPE,
             GQA_HEADS * HEAD_DIM, GQA_KV_HEADS * HEAD_DIM, GQA_KV_HEADS * HEAD_DIM)
    offs = np.concatenate([[0], np.cumsum(sizes)])
    (wqa, wka, wva, wcq, wckv, wkpe, wqc, wkc, wvc) = [wi[:, offs[i]:offs[i + 1]] for i in range(9)]
    z = lambda n: jnp.zeros((K, n), F32)

    w_na = jnp.concatenate([wqa * NA_QSCALE, wka, wva], axis=1)
    kpe_tile = jnp.concatenate([z(MLA_NOPE), wkpe, z(LANES - MLA_QK)], axis=1)
    kpe_rot = jnp.concatenate([z(MLA_NOPE), _rot32(wkpe), z(LANES - MLA_QK)], axis=1)
    w_mla = jnp.concatenate([wcq, wckv, kpe_tile, kpe_rot], axis=1)

    qc_tiles, qc_rots = [], []
    for h in range(GQA_HEADS):
        wh = wqc[:, h * HEAD_DIM:(h + 1) * HEAD_DIM]
        pad = (lambda a: jnp.concatenate([a, z(HEAD_DIM)], axis=1)) if h // GQA_GROUP == 0 else \
              (lambda a: jnp.concatenate([z(HEAD_DIM), a], axis=1))
        qc_tiles.append(pad(wh))
        qc_rots.append(pad(_rot64(wh)))
    w_qc = jnp.concatenate(qc_tiles + qc_rots, axis=1)
    w_kc = jnp.concatenate(
        [wkc] + [_rot64(wkc[:, h * HEAD_DIM:(h + 1) * HEAD_DIM]) for h in range(GQA_KV_HEADS)], axis=1)

    uq = w_uq[l]
    zq = lambda n: jnp.zeros((MLA_Q_RANK, n), F32)
    uq_tiles, uq_rots = [], []
    for h in range(MLA_HEADS):
        wh = uq[:, h * MLA_QK:(h + 1) * MLA_QK]
        uq_tiles.append(jnp.concatenate([wh, zq(LANES - MLA_QK)], axis=1))
        uq_rots.append(jnp.concatenate([zq(MLA_NOPE), _rot32(wh[:, MLA_NOPE:]), zq(LANES - MLA_QK)], axis=1))
    w_uq_ext = jnp.concatenate(uq_tiles + uq_rots, axis=1)

    ukv = w_ukv[l]
    zk = jnp.zeros((MLA_KV_RANK, LANES - MLA_NOPE), F32)
    per = MLA_NOPE + MLA_V
    w_uk = jnp.concatenate(
        [jnp.concatenate([ukv[:, h * per:h * per + MLA_NOPE], zk], axis=1) for h in range(MLA_HEADS)], axis=1)
    w_uv = jnp.concatenate([ukv[:, h * per + MLA_NOPE:(h + 1) * per] for h in range(MLA_HEADS)], axis=1)

    two = lambda g: jnp.concatenate([g, g])
    row = lambda g: g.reshape(1, -1).astype(F32)
    return {
        "g_attn": row(attn_norm[l]),
        "w_na": w_na.astype(BF16), "w_mla": w_mla.astype(BF16), "w_qc": w_qc.astype(BF16),
        "w_kc": w_kc.astype(BF16), "w_vc": wvc.astype(BF16),
        "g_qa": row(q_a_norm[l]), "g_kva": row(kv_a_norm[l]),
        "w_uq": w_uq_ext.astype(BF16), "w_uk": w_uk.astype(BF16), "w_uv": w_uv.astype(BF16),
        "g_q": jnp.stack([two(q_norm_c[l]), two(_perm64(q_norm_c[l]))]).astype(F32),
        "g_k": jnp.stack([two(k_norm_c[l]), two(_perm64(k_norm_c[l]))]).astype(F32),
        "g_a": row(g_out_a[l]), "g_b": row(g_out_b[l]), "g_c": row(g_out_c[l]),
        "w_out": w_out[l].astype(BF16), "g_ffn": row(ffn_norm[l]),
        "w_gate": w_gate[l].astype(BF16), "w_up": w_up[l].astype(BF16), "w_down": w_down[l].astype(BF16),
    }


def _rope_tables(S):
    tok = jnp.arange(S)
    t = tok.astype(F32)
    row = (tok // GRID_W).astype(F32)
    col = (tok % GRID_W).astype(F32)
    half = MLA_ROPE // 2
    inv = ROPE_BASE ** (-(jnp.arange(half, dtype=F32) * 2.0) / MLA_ROPE)

    def cs(pos):
        ang = pos[:, None] * inv[None, :]
        c, s = jnp.cos(ang), jnp.sin(ang)
        return jnp.concatenate([c, c], axis=1), jnp.concatenate([s, s], axis=1)

    ct, st = cs(t)
    ones = jnp.ones((S, MLA_NOPE), F32)
    zeros = jnp.zeros((S, MLA_NOPE), F32)
    pad1 = jnp.ones((S, LANES - MLA_QK), F32)
    pad0 = jnp.zeros((S, LANES - MLA_QK), F32)
    cr, sr = cs(row)
    cc, sc = cs(col)
    ca = jnp.concatenate([cr, cc], axis=1)
    sa = jnp.concatenate([sr, sc], axis=1)
    return {
        "cos_m": jnp.concatenate([ones, ct, pad1], axis=1),
        "sin_m": jnp.concatenate([zeros, st, pad0], axis=1),
        "cos_a": jnp.concatenate([ca, ca], axis=1),
        "sin_a": jnp.concatenate([sa, sa], axis=1),
    }


def _na_bias_table(rpb_l):
    c = np.arange(GRID_W)
    cstart = np.clip(c - NA_COLS // 2, 0, GRID_W - NA_COLS)
    cp = np.arange(GRID_W)
    dc = cp[None, :] - c[:, None] + (NA_COLS - 1)
    valid = (cp[None, :] >= cstart[:, None]) & (cp[None, :] < cstart[:, None] + NA_COLS)
    onehot = ((dc[:, :, None] == np.arange(2 * NA_COLS - 1)) & valid[:, :, None]).astype(np.float32)
    t = jnp.einsum("hrd,cpd->hrcp", rpb_l.astype(F32), onehot, precision=lax.Precision.HIGHEST)
    t = jnp.where(valid[None, None], t, MASK_VALUE)
    b = jnp.stack([t[:, NA_ROWS - 1 - d:2 * NA_ROWS - 1 - d] for d in range(NA_ROWS)], axis=1)
    b = b.transpose(0, 1, 3, 2, 4).reshape(NA_HEADS // 2, 2, NA_ROWS, GRID_W, NA_KEYS)
    return b.transpose(0, 2, 1, 3, 4).reshape(NA_HEADS // 2, NA_ROWS, 2 * GRID_W, NA_KEYS)


MLA_K_TILES = tuple(range(MLA_HEADS))
MLA_V_ROWS = tuple(range(MLA_HEADS))
GQA_K_TILES = (0,) * GQA_HEADS
GQA_V_ROWS = tuple(h // GQA_GROUP for h in range(GQA_HEADS))


def _trunk(x, layers, biases, tabs, g_final):
    n_layers = len(layers)
    for l, lw in enumerate(layers):
        na, mq, mk, mvt, gq, gk, gvt = _pre_call(x, lw, tabs)
        oa = _na_call(na, biases[l])
        ob = _attn_call(mq, mk, mvt, k_tiles=MLA_K_TILES, v_rows=MLA_V_ROWS, name="mla")
        oc = _attn_call(gq, gk, gvt, k_tiles=GQA_K_TILES, v_rows=GQA_V_ROWS, name="gqa")
        x = _post_call(x, oa, ob, oc, lw, g_final, final=(l == n_layers - 1))
    return x


def kernel(x_prompt, x_sample, attn_norm, w_in, q_a_norm, kv_a_norm, w_uq, w_ukv, q_norm_c, k_norm_c, rpb, g_out_a, g_out_b, g_out_c, w_out, ffn_norm, w_gate, w_up, w_down, final_norm):
    depth = w_in.shape[0]
    layers = [_layer_weights(l, attn_norm, w_in, q_a_norm, kv_a_norm, w_uq, w_ukv, q_norm_c,
                             k_norm_c, g_out_a, g_out_b, g_out_c, w_out, ffn_norm, w_gate,
                             w_up, w_down) for l in range(depth)]
    biases = [_na_bias_table(rpb[l]) for l in range(depth)]
    g_final = final_norm.reshape(1, -1).astype(F32)
    tabs = _rope_tables(max(x_prompt.shape[1], x_sample.shape[1]))
    outs = []
    for x in (x_prompt, x_sample):
        outs.append(_trunk(x, layers, biases, tabs, g_final))
    return tuple(outs)
```
